```python
import math
import jax, jax.numpy as jnp
from jax import lax
import numpy as np

D_MODEL = 2048
BATCH = 16
SEQ = 256
DEPTH = 4
DEC_BATCH = 2
DEC_SEQ = 2048
PAST_LEN = 512

GRID_W = 64
N_MIXERS = 3
N_CONV_LAYERS = (DEPTH + 2) // 3
N_SSD_LAYERS = (DEPTH + 1) // 3
N_ATTN_LAYERS = DEPTH // 3
EPS = 1e-6

CONV_W = 31

SSM_EXPAND = 2
D_INNER = SSM_EXPAND * D_MODEL
SSM_HEAD_DIM = 64
SSM_HEADS = D_INNER // SSM_HEAD_DIM
SSM_GROUPS = 8
SSM_STATE = 128
SSM_CONV_W = 5
SSM_CHUNK = 128
SSM_CONV_DIM = D_INNER + 2 * SSM_GROUPS * SSM_STATE
SSM_IN_W = D_INNER + SSM_CONV_DIM + 2 * SSM_HEADS

HEAD_DIM = 128
N_HEADS = D_MODEL // HEAD_DIM
N_KV_HEADS = 4
KV_GROUP = N_HEADS // N_KV_HEADS
ROPE_PAIRS_PER_AXIS = HEAD_DIM // 4
ROPE_THETA = 10000.0
Q_BLOCK = 128
QKV_W = (N_HEADS + 2 * N_KV_HEADS) * HEAD_DIM

D_FF = 5632
FFN_CONV_W = 3

kernel_name = 'hybrid_diffusion_prefix_trunk_step'


def rms_norm(x, g):
    xf = x.astype(jnp.float32)
    y = xf * lax.rsqrt(jnp.mean(xf * xf, axis=-1, keepdims=True) + EPS)
    return (y * g.astype(jnp.float32)).astype(x.dtype)


def layer_norm(x, g, b):
    xf = x.astype(jnp.float32)
    xc = xf - jnp.mean(xf, axis=-1, keepdims=True)
    y = xc * lax.rsqrt(jnp.mean(xc * xc, axis=-1, keepdims=True) + EPS)
    return (y * g.astype(jnp.float32) + b.astype(jnp.float32)).astype(x.dtype)


def dwconv(x, w, b):
    width = w.shape[0]
    left = (width - 1) // 2
    y = lax.conv_general_dilated(
        x, w[:, None, :].astype(x.dtype), window_strides=(1,),
        padding=[(left, width - 1 - left)],
        dimension_numbers=('NWC', 'WIO', 'NWC'),
        feature_group_count=x.shape[-1])
    return y + b.astype(x.dtype)


def modulation(cond, w_mod, b_mod):
    m = jax.nn.silu(cond) @ w_mod + b_mod
    return jnp.split(m[..., None, :], 6, axis=-1)


def conformer_conv(h, w_pw1, b_pw1, w_dw, b_dw, ln_g, ln_b, w_pw2, b_pw2):
    u = h @ w_pw1 + b_pw1
    a, g = jnp.split(u, 2, axis=-1)
    u = a * jax.nn.sigmoid(g)
    u = dwconv(u, w_dw, b_dw)
    u = jax.nn.silu(layer_norm(u, ln_g, ln_b))
    return u @ w_pw2 + b_pw2


def ssd_scan(x, dt, a, bm, cm, h0):
    bsz, length, nh, p = x.shape
    g, n = bm.shape[-2:]
    r = nh // g
    nc = length // SSM_CHUNK
    L = SSM_CHUNK
    xc = (x * dt[..., None]).reshape(bsz, nc, L, g, r, p)
    a_cs = jnp.cumsum((dt * a).reshape(bsz, nc, L, g, r), axis=2)
    bc = bm.reshape(bsz, nc, L, g, n)
    cc = cm.reshape(bsz, nc, L, g, n)
    seg = a_cs[:, :, :, None] - a_cs[:, :, None, :]
    mask = jnp.tril(jnp.ones((L, L), dtype=bool))[None, None, :, :, None, None]
    decay = jnp.exp(jnp.where(mask, seg, -jnp.inf))
    scores = jnp.einsum('bclgn,bcsgn->bclsg', cc, bc)
    y_diag = jnp.einsum('bclsgr,bcsgrp->bclgrp', scores[..., None] * decay, xc)
    decay_to_end = jnp.exp(a_cs[:, :, -1:] - a_cs)
    chunk_states = jnp.einsum('bclgn,bclgr,bclgrp->bcgrpn', bc, decay_to_end, xc)
    chunk_decay = jnp.exp(a_cs[:, :, -1])

    def step(state, inp):
        cs, cd = inp
        return state * cd[..., None, None] + cs, state

    final, prev = lax.scan(step, h0.reshape(bsz, g, r, p, n),
                           (jnp.moveaxis(chunk_states, 1, 0), jnp.moveaxis(chunk_decay, 1, 0)))
    prev = jnp.moveaxis(prev, 0, 1)
    y_off = jnp.einsum('bclgn,bcgrpn,bclgr->bclgrp', cc, prev, jnp.exp(a_cs))
    y = (y_diag + y_off).reshape(bsz, length, nh, p)
    return y, final.reshape(bsz, nh, p, n)


def ssd_mixer(h, h0, w_in, w_conv, b_conv, dt_bias, a_log, d_skip, norm_g, w_out):
    bsz, length, _ = h.shape
    z, xbc, dt = jnp.split(h @ w_in, [D_INNER, D_INNER + SSM_CONV_DIM], axis=-1)
    xbc = jax.nn.silu(dwconv(xbc, w_conv, b_conv)).astype(jnp.float32)
    xs, bm, cm = jnp.split(xbc, [D_INNER, D_INNER + SSM_GROUPS * SSM_STATE], axis=-1)
    xs = xs.reshape(bsz, length, SSM_HEADS, SSM_HEAD_DIM)
    bm = bm.reshape(bsz, length, SSM_GROUPS, SSM_STATE)
    cm = cm.reshape(bsz, length, SSM_GROUPS, SSM_STATE)
    dt = jax.nn.softplus(dt.astype(jnp.float32).reshape(bsz, length, 2, SSM_HEADS)
                         + dt_bias.astype(jnp.float32))
    a = -jnp.exp(a_log.astype(jnp.float32))
    h0 = h0.astype(jnp.float32)
    y_f, s_f = ssd_scan(xs, dt[:, :, 0], a[0], bm, cm, h0[:, 0])
    y_b, s_b = ssd_scan(jnp.flip(xs, 1), jnp.flip(dt[:, :, 1], 1), a[1],
                        jnp.flip(bm, 1), jnp.flip(cm, 1), h0[:, 1])
    y = y_f + jnp.flip(y_b, 1) + d_skip.astype(jnp.float32)[:, None] * xs
    y = y.reshape(bsz, length, D_INNER) * jax.nn.silu(z.astype(jnp.float32))
    y = rms_norm(y, norm_g).astype(h.dtype)
    return y @ w_out, jnp.stack([s_f, s_b], axis=1)


def qkv_heads(h, w_qkv, q_norm, k_norm):
    bsz, length, _ = h.shape
    q, k, v = jnp.split(h @ w_qkv, [N_HEADS * HEAD_DIM, (N_HEADS + N_KV_HEADS) * HEAD_DIM], axis=-1)
    q = rms_norm(q.reshape(bsz, length, N_HEADS, HEAD_DIM), q_norm)
    k = rms_norm(k.reshape(bsz, length, N_KV_HEADS, HEAD_DIM), k_norm)
    v = v.reshape(bsz, length, N_KV_HEADS, HEAD_DIM)
    return q, k, v


def axial_rope_tables(rows):
    pos_row = jnp.repeat(jnp.arange(rows, dtype=jnp.float32), GRID_W)
    pos_col = jnp.tile(jnp.arange(GRID_W, dtype=jnp.float32), rows)
    inv = ROPE_THETA ** (-jnp.arange(ROPE_PAIRS_PER_AXIS, dtype=jnp.float32) / ROPE_PAIRS_PER_AXIS)
    ang = jnp.concatenate([pos_row[:, None] * inv, pos_col[:, None] * inv], axis=-1)
    return jnp.cos(ang)[:, None, :], jnp.sin(ang)[:, None, :]


def apply_rope(x, cos, sin):
    xf = x.astype(jnp.float32).reshape(*x.shape[:-1], HEAD_DIM // 2, 2)
    x1, x2 = xf[..., 0], xf[..., 1]
    out = jnp.stack([x1 * cos - x2 * sin, x1 * sin + x2 * cos], axis=-1)
    return out.reshape(x.shape).astype(x.dtype)


def block_attention(q, k, v):
    bsz, lq = q.shape[:2]
    nblk = lq // Q_BLOCK
    qb = jnp.moveaxis(q.reshape(bsz, nblk, Q_BLOCK, N_KV_HEADS, KV_GROUP, HEAD_DIM), 1, 0)
    scale = HEAD_DIM ** -0.5

    def one_block(qblk):
        s = jnp.einsum('bqkgd,bskd->bkgqs', qblk, k, preferred_element_type=jnp.float32) * scale
        p = jax.nn.softmax(s, axis=-1).astype(v.dtype)
        return jnp.einsum('bkgqs,bskd->bqkgd', p, v)

    o = lax.map(one_block, qb)
    return jnp.moveaxis(o, 0, 1).reshape(bsz, lq, N_HEADS * HEAD_DIM)


def conv_ffn(h, w_up, w_dw, b_dw, w_down):
    u = dwconv(h @ w_up, w_dw, b_dw)
    a, g = jnp.split(u, 2, axis=-1)
    return (jax.nn.silu(g) * a) @ w_down


def setup_inputs(seed: int = 0) -> dict:
    key = jax.random.key(seed)
    keys = iter(jax.random.split(key, 64))

    def nrm(shape, scale=1.0):
        return jax.random.normal(next(keys), shape, jnp.float32) * scale

    def gain(shape):
        return 1.0 + nrm(shape, 0.02)

    D = D_MODEL
    inv = D ** -0.5
    dt0 = jnp.exp(jax.random.uniform(next(keys), (N_SSD_LAYERS, 2, SSM_HEADS), jnp.float32,
                                     math.log(1e-3), math.log(1e-1)))
    dt_bias = dt0 + jnp.log(-jnp.expm1(-dt0))
    a_log = jnp.log(jax.random.uniform(next(keys), (N_SSD_LAYERS, 2, SSM_HEADS), jnp.float32, 1.0, 16.0))
    return {
        'x_prompt': nrm((BATCH, SEQ, D)),
        'x_sample': nrm((DEC_BATCH, DEC_SEQ, D)),
        'c': nrm((DEC_BATCH, D)),
        'state_ssd': nrm((DEC_BATCH, N_SSD_LAYERS, 2, SSM_HEADS, SSM_HEAD_DIM, SSM_STATE), 0.5),
        'cache_k': nrm((DEC_BATCH, N_ATTN_LAYERS, PAST_LEN, N_KV_HEADS, HEAD_DIM)),
        'cache_v': nrm((DEC_BATCH, N_ATTN_LAYERS, PAST_LEN, N_KV_HEADS, HEAD_DIM)),
        'c_ctx': nrm((D,)),
        'w_mod': nrm((DEPTH, D, 6 * D), 0.5 * inv),
        'b_mod': nrm((DEPTH, 6 * D), 0.02),
        'norm_pre': gain((DEPTH, 2, D)),
        'norm_post': gain((DEPTH, 2, D)),
        'cv_w_pw1': nrm((N_CONV_LAYERS, D, 2 * D), inv),
        'cv_b_pw1': nrm((N_CONV_LAYERS, 2 * D), 0.02),
        'cv_w_dw': nrm((N_CONV_LAYERS, CONV_W, D), CONV_W ** -0.5),
        'cv_b_dw': nrm((N_CONV_LAYERS, D), 0.02),
        'cv_ln_g': gain((N_CONV_LAYERS, D)),
        'cv_ln_b': nrm((N_CONV_LAYERS, D), 0.02),
        'cv_w_pw2': nrm((N_CONV_LAYERS, D, D), inv),
        'cv_b_pw2': nrm((N_CONV_LAYERS, D), 0.02),
        'ssd_w_in': nrm((N_SSD_LAYERS, D, SSM_IN_W), inv),
        'ssd_w_conv': nrm((N_SSD_LAYERS, SSM_CONV_W, SSM_CONV_DIM), SSM_CONV_W ** -0.5),
        'ssd_b_conv': nrm((N_SSD_LAYERS, SSM_CONV_DIM), 0.02),
        'ssd_dt_bias': dt_bias,
        'ssd_a_log': a_log,
        'ssd_d': gain((N_SSD_LAYERS, SSM_HEADS)),
        'ssd_norm_g': gain((N_SSD_LAYERS, D_INNER)),
        'ssd_w_out': nrm((N_SSD_LAYERS, D_INNER, D), D_INNER ** -0.5),
        'attn_w_qkv': nrm((N_ATTN_LAYERS, D, QKV_W), inv),
        'attn_q_norm': gain((N_ATTN_LAYERS, HEAD_DIM)),
        'attn_k_norm': gain((N_ATTN_LAYERS, HEAD_DIM)),
        'attn_w_o': nrm((N_ATTN_LAYERS, N_HEADS * HEAD_DIM, D), (N_HEADS * HEAD_DIM) ** -0.5),
        'ffn_w_up': nrm((DEPTH, D, 2 * D_FF), inv),
        'ffn_w_dw': nrm((DEPTH, FFN_CONV_W, 2 * D_FF), FFN_CONV_W ** -0.5),
        'ffn_b_dw': nrm((DEPTH, 2 * D_FF), 0.02),
        'ffn_w_down': nrm((DEPTH, D_FF, D), D_FF ** -0.5),
    }


def reference(x_prompt, x_sample, c, state_ssd, cache_k, cache_v, c_ctx, w_mod, b_mod,
              norm_pre, norm_post, cv_w_pw1, cv_b_pw1, cv_w_dw, cv_b_dw, cv_ln_g, cv_ln_b,
              cv_w_pw2, cv_b_pw2, ssd_w_in, ssd_w_conv, ssd_b_conv, ssd_dt_bias, ssd_a_log,
              ssd_d, ssd_norm_g, ssd_w_out, attn_w_qkv, attn_q_norm, attn_k_norm, attn_w_o,
              ffn_w_up, ffn_w_dw, ffn_b_dw, ffn_w_down):
    rows = x_sample.shape[1] // GRID_W
    cos, sin = axial_rope_tables(rows)
    yp, ys = x_prompt, x_sample
    new_ssd, new_k, new_v = [], [], []
    for i in range(DEPTH):
        kind, j = i % N_MIXERS, i // N_MIXERS
        sh_mp, sc_mp, ga_mp, sh_fp, sc_fp, ga_fp = modulation(c_ctx, w_mod[i], b_mod[i])
        sh_ms, sc_ms, ga_ms, sh_fs, sc_fs, ga_fs = modulation(c, w_mod[i], b_mod[i])
        hp = rms_norm(yp, norm_pre[i, 0]) * (1 + sc_mp) + sh_mp
        hs = rms_norm(ys, norm_pre[i, 0]) * (1 + sc_ms) + sh_ms
        if kind == 0:
            cargs = (cv_w_pw1[j], cv_b_pw1[j], cv_w_dw[j], cv_b_dw[j], cv_ln_g[j], cv_ln_b[j],
                     cv_w_pw2[j], cv_b_pw2[j])
            op = conformer_conv(hp, *cargs)
            os_ = conformer_conv(hs, *cargs)
        elif kind == 1:
            sargs = (ssd_w_in[j], ssd_w_conv[j], ssd_b_conv[j], ssd_dt_bias[j], ssd_a_log[j],
                     ssd_d[j], ssd_norm_g[j], ssd_w_out[j])
            zero_state = jnp.zeros((hp.shape[0], 2, SSM_HEADS, SSM_HEAD_DIM, SSM_STATE), jnp.float32)
            op, st = ssd_mixer(hp, zero_state, *sargs)
            new_ssd.append(st)
            os_, _ = ssd_mixer(hs, state_ssd[:, j], *sargs)
        else:
            qp, kp, vp = qkv_heads(hp, attn_w_qkv[j], attn_q_norm[j], attn_k_norm[j])
            op = block_attention(qp, kp, vp) @ attn_w_o[j]
            new_k.append(kp)
            new_v.append(vp)
            qs, ks, vs = qkv_heads(hs, attn_w_qkv[j], attn_q_norm[j], attn_k_norm[j])
            qs = apply_rope(qs, cos, sin)
            ks = apply_rope(ks, cos, sin)
            k_all = jnp.concatenate([ks, cache_k[:, j].astype(ks.dtype)], axis=1)
            v_all = jnp.concatenate([vs, cache_v[:, j].astype(vs.dtype)], axis=1)
            os_ = block_attention(qs, k_all, v_all) @ attn_w_o[j]
        yp = yp + ga_mp * rms_norm(op, norm_post[i, 0])
        ys = ys + ga_ms * rms_norm(os_, norm_post[i, 0])
        fargs = (ffn_w_up[i], ffn_w_dw[i], ffn_b_dw[i], ffn_w_down[i])
        hp = rms_norm(yp, norm_pre[i, 1]) * (1 + sc_fp) + sh_fp
        hs = rms_norm(ys, norm_pre[i, 1]) * (1 + sc_fs) + sh_fs
        yp = yp + ga_fp * rms_norm(conv_ffn(hp, *fargs), norm_post[i, 1])
        ys = ys + ga_fs * rms_norm(conv_ffn(hs, *fargs), norm_post[i, 1])
    new_state_ssd = jnp.stack(new_ssd, axis=1)
    new_cache_k = jnp.stack(new_k, axis=1)
    new_cache_v = jnp.stack(new_v, axis=1)
    return (yp, ys, new_state_ssd, new_cache_k, new_cache_v)
```

```python
import functools
import math

import jax
import jax.numpy as jnp
from jax import lax
from jax.experimental import pallas as pl
from jax.experimental.pallas import tpu as pltpu

f32 = jnp.float32
bf16 = jnp.bfloat16
HIGHEST = lax.Precision.HIGHEST

EPS = 1e-6
GRID_W = 64
ROPE_THETA = 10000.0
SSM_CHUNK = 128
N_MIXERS = 3
V7X_VMEM_LIMIT_BYTES = 56 * 1024 * 1024
COND_ROWS = 8


def _params(n_axes, vmem=V7X_VMEM_LIMIT_BYTES):
    return pltpu.CompilerParams(dimension_semantics=("arbitrary",) * n_axes,
                                vmem_limit_bytes=vmem)


def _tile(dim, pref):
    return pref if dim % pref == 0 else dim


def _sigmoid(x):
    return 1.0 / (1.0 + jnp.exp(-x))


def _silu(x):
    return x * _sigmoid(x)


def _rms(x, g):
    return x * lax.rsqrt(jnp.mean(x * x, axis=-1, keepdims=True) + EPS) * g


def _mod_kernel(c_ref, w_ref, b_ref, o_ref):
    s = _silu(c_ref[...]).astype(bf16)
    o_ref[...] = jnp.dot(s, w_ref[...].astype(bf16), preferred_element_type=f32) + b_ref[...]


def _modulation(cond, w_mod, b_mod):
    depth, d, n = w_mod.shape
    tn = _tile(n, 1024)
    return pl.pallas_call(
        _mod_kernel,
        grid=(depth, n // tn),
        in_specs=[pl.BlockSpec((COND_ROWS, d), lambda l, j: (0, 0)),
                  pl.BlockSpec((None, d, tn), lambda l, j: (l, 0, j)),
                  pl.BlockSpec((None, 1, tn), lambda l, j: (l, 0, j))],
        out_specs=pl.BlockSpec((None, COND_ROWS, tn), lambda l, j: (l, 0, j)),
        out_shape=jax.ShapeDtypeStruct((depth, COND_ROWS, n), f32),
        compiler_params=_params(2), name="modulation",
    )(cond, w_mod, b_mod.reshape(depth, 1, n))


def _prenorm_kernel(y_ref, g_ref, mod_ref, h_ref, *, sc, sh):
    yn = _rms(y_ref[...], g_ref[...])
    h_ref[...] = (yn * (1.0 + mod_ref[sc:sc + 1, :]) + mod_ref[sh:sh + 1, :]).astype(h_ref.dtype)


def _resid_kernel(y_ref, o_ref, gpost_ref, moda_ref, *rest, ga, nxt):
    y = y_ref[...] + moda_ref[ga:ga + 1, :] * _rms(o_ref[...], gpost_ref[...])
    if nxt is None:
        (ynew_ref,) = rest
        ynew_ref[...] = y
    else:
        gpre_ref, modb_ref, ynew_ref, h_ref = rest
        sc, sh = nxt
        ynew_ref[...] = y
        yn = _rms(y, gpre_ref[...])
        h_ref[...] = (yn * (1.0 + modb_ref[sc:sc + 1, :]) + modb_ref[sh:sh + 1, :]).astype(h_ref.dtype)


class _Rows:
    def __init__(self, mp, ms, seq_p, seq_s):
        self.mp, self.ms, self.m = mp, ms, mp + ms
        self.seq_p, self.seq_s = seq_p, seq_s

    def cond_of_tile(self, tm):
        assert self.mp % tm == 0 and self.seq_s % tm == 0
        npt, per = self.mp // tm, self.seq_s // tm
        return lambda i: jnp.where(i < npt, 0, 1 + (i - npt) // per)


def _prenorm(y, g_all, g_idx, mod, layer, sc, sh, rows):
    m, d = y.shape
    tm = _tile(rows.seq_s, 256)
    cond = rows.cond_of_tile(tm)
    return pl.pallas_call(
        functools.partial(_prenorm_kernel, sc=sc, sh=sh),
        grid=(m // tm,),
        in_specs=[pl.BlockSpec((tm, d), lambda i: (i, 0)),
                  pl.BlockSpec((None, 1, d), lambda i: (g_idx, 0, 0)),
                  pl.BlockSpec((None, 6, d), lambda i: (layer * COND_ROWS + cond(i), 0, 0))],
        out_specs=pl.BlockSpec((tm, d), lambda i: (i, 0)),
        out_shape=jax.ShapeDtypeStruct((m, d), bf16),
        compiler_params=_params(1), name="prenorm",
    )(y, g_all, mod)


def _resid(y, o, gpost_all, gpost_idx, mod, layer, ga, rows, nxt=None):
    m, d = y.shape
    tm = _tile(rows.seq_s, 256)
    cond = rows.cond_of_tile(tm)
    row_spec = pl.BlockSpec((tm, d), lambda i: (i, 0))
    in_specs = [row_spec, row_spec,
                pl.BlockSpec((None, 1, d), lambda i: (gpost_idx, 0, 0)),
                pl.BlockSpec((None, 6, d), lambda i: (layer * COND_ROWS + cond(i), 0, 0))]
    args = [y, o, gpost_all, mod]
    if nxt is None:
        out_specs, out_shape, kn = row_spec, jax.ShapeDtypeStruct((m, d), f32), None
    else:
        gpre_all, gpre_idx, nlayer, sc, sh = nxt
        in_specs += [pl.BlockSpec((None, 1, d), lambda i: (gpre_idx, 0, 0)),
                     pl.BlockSpec((None, 6, d), lambda i: (nlayer * COND_ROWS + cond(i), 0, 0))]
        args += [gpre_all, mod]
        out_specs = [row_spec, row_spec]
        out_shape = [jax.ShapeDtypeStruct((m, d), f32), jax.ShapeDtypeStruct((m, d), bf16)]
        kn = (sc, sh)
    return pl.pallas_call(
        functools.partial(_resid_kernel, ga=ga, nxt=kn),
        grid=(m // tm,), in_specs=in_specs, out_specs=out_specs, out_shape=out_shape,
        compiler_params=_params(1), name="resid",
    )(*args)


def _fmm_kernel(*refs, n_slab, n_extra, n_out, epilogue):
    x_ref = refs[0]
    w_refs = refs[1:1 + n_slab]
    e_refs = refs[1 + n_slab:1 + n_slab + n_extra]
    o_refs = refs[1 + n_slab + n_extra:1 + n_slab + n_extra + n_out]
    wb_refs = refs[1 + n_slab + n_extra + n_out:]
    i = pl.program_id(1)

    @pl.when(i == 0)
    def _():
        for w_ref, wb_ref in zip(w_refs, wb_refs):
            wb_ref[...] = w_ref[...].astype(bf16)

    x = x_ref[...]
    accs = [jnp.dot(x, wb_ref[...], preferred_element_type=f32) for wb_ref in wb_refs]
    epilogue(accs, e_refs, o_refs, i)


def _fused_matmul(x, w_all, layer, slab_offs, *, tm, tn, n_tiles, extras, outs, epilogue, name,
                  row_off=0, n_row_tiles=None):
    m, k = x.shape
    if n_row_tiles is None:
        n_row_tiles = m // tm
    in_specs = [pl.BlockSpec((tm, k), lambda j, i: (i + row_off, 0))]
    args = [x]
    for off in slab_offs:
        in_specs.append(pl.BlockSpec((None, k, tn), lambda j, i, off=off: (layer, 0, off + j)))
        args.append(w_all)
    for arr, spec in extras:
        in_specs.append(spec)
        args.append(arr)
    return pl.pallas_call(
        functools.partial(_fmm_kernel, n_slab=len(slab_offs), n_extra=len(extras),
                          n_out=len(outs), epilogue=epilogue),
        grid=(n_tiles, n_row_tiles),
        in_specs=in_specs,
        out_specs=[spec for _, spec in outs],
        out_shape=[sds for sds, _ in outs],
        scratch_shapes=[pltpu.VMEM((k, tn), bf16) for _ in slab_offs],
        compiler_params=_params(2), name=name,
    )(*args)


def _bias_spec(layer, tn, off=0):
    return pl.BlockSpec((None, 1, tn), lambda j, i: (layer, 0, off + j))


def _out_tile(m, n, tm, tn, dtype):
    return (jax.ShapeDtypeStruct((m, n), dtype), pl.BlockSpec((tm, tn), lambda j, i: (i, j)))


def _epi_plain(accs, e_refs, o_refs, i):
    o_refs[0][...] = accs[0].astype(o_refs[0].dtype)


def _epi_bias(accs, e_refs, o_refs, i):
    o_refs[0][...] = (accs[0] + e_refs[0][...]).astype(o_refs[0].dtype)


def _dwconv_rows(u, w, seq):
    tm = u.shape[0]
    width = w.shape[0]
    left = (width - 1) // 2
    pos = lax.broadcasted_iota(jnp.int32, u.shape, 0) & (seq - 1)
    acc = u * w[left:left + 1, :]
    for k in range(width):
        d = k - left
        if d == 0:
            continue
        shifted = pltpu.roll(u, (-d) % tm, axis=0)
        valid = (pos >= -d) if d < 0 else (pos < seq - d)
        acc = acc + jnp.where(valid, shifted, 0.0) * w[k:k + 1, :]
    return acc


def _ffn_up_epilogue(accs, e_refs, o_refs, i, *, n_prompt_tiles, seq_p, seq_s):
    wa_ref, wg_ref, ba_ref, bg_ref = e_refs
    seq = jnp.where(i < n_prompt_tiles, seq_p, seq_s)
    a = _dwconv_rows(accs[0], wa_ref[...], seq) + ba_ref[...]
    g = _dwconv_rows(accs[1], wg_ref[...], seq) + bg_ref[...]
    o_refs[0][...] = (_silu(g) * a).astype(o_refs[0].dtype)


def _conv_ffn(h, layer, w_up, w_dw, b_dw, w_down, rows):
    m, d = h.shape
    dff = w_down.shape[1]
    depth = w_up.shape[0]
    tm = rows.seq_s
    tn = _tile(dff, 256)
    nt = dff // tn
    width = w_dw.shape[1]
    b3 = b_dw.reshape(depth, 1, 2 * dff)
    dw_spec = lambda off: pl.BlockSpec((None, width, tn), lambda j, i: (layer, 0, off + j))
    act = _fused_matmul(
        h, w_up, layer, [0, nt], tm=tm, tn=tn, n_tiles=nt,
        extras=[(w_dw, dw_spec(0)), (w_dw, dw_spec(nt)), (b3, _bias_spec(layer, tn)), (b3, _bias_spec(layer, tn, nt))],
        outs=[_out_tile(m, dff, tm, tn, bf16)],
        epilogue=functools.partial(_ffn_up_epilogue, n_prompt_tiles=rows.mp // tm,
                                   seq_p=rows.seq_p, seq_s=rows.seq_s),
        name="ffn_up")[0]
    tm2, tn2 = _tile(m, 512), _tile(d, 512)
    return _fused_matmul(act, w_down, layer, [0], tm=tm2, tn=tn2, n_tiles=d // tn2, extras=[],
                         outs=[_out_tile(m, d, tm2, tn2, f32)], epilogue=_epi_plain, name="ffn_down")[0]


def _glu_epilogue(accs, e_refs, o_refs, i):
    a = accs[0] + e_refs[0][...]
    g = accs[1] + e_refs[1][...]
    o_refs[0][...] = (a * _sigmoid(g)).astype(o_refs[0].dtype)


CONV_HALO = 16
CONV_LANES = 128
CONV_ROWS = 128


def _cv_mid_kernel(u_ref, prev_ref, next_ref, w_ref, b_ref, g_ref, beta_ref, o_ref, pad_ref, conv_ref,
                   *, blocks_p, per_seq_p, per_seq_s):
    i = pl.program_id(0)
    rb, d = u_ref.shape
    width = w_ref.shape[0]
    left = (width - 1) // 2
    per = jnp.where(i < blocks_p, per_seq_p, per_seq_s)
    k_in_seq = jnp.where(i < blocks_p, i, i - blocks_p) % per
    first = k_in_seq == 0
    last = k_in_seq == per - 1
    pad_ref[0:CONV_HALO, :] = jnp.where(first, 0.0, prev_ref[...])
    pad_ref[CONV_HALO:CONV_HALO + rb, :] = u_ref[...]
    pad_ref[CONV_HALO + rb:CONV_HALO + rb + CONV_HALO, :] = jnp.where(last, 0.0, next_ref[...])

    def lane_chunk(cidx, carry):
        lo = pl.multiple_of(cidx * CONV_LANES, CONV_LANES)
        wc = w_ref[:, pl.ds(lo, CONV_LANES)]
        for r0 in range(0, rb, CONV_ROWS):
            acc = jnp.zeros((CONV_ROWS, CONV_LANES), f32)
            for k in range(width):
                start = CONV_HALO - left + k + r0
                acc = acc + pad_ref[pl.ds(start, CONV_ROWS), pl.ds(lo, CONV_LANES)] * wc[k:k + 1, :]
            conv_ref[pl.ds(r0, CONV_ROWS), pl.ds(lo, CONV_LANES)] = acc
        return carry

    lax.fori_loop(0, d // CONV_LANES, lane_chunk, 0)
    u = conv_ref[...] + b_ref[...]
    mu = jnp.mean(u, axis=-1, keepdims=True)
    uc = u - mu
    v = uc * lax.rsqrt(jnp.mean(uc * uc, axis=-1, keepdims=True) + EPS) * g_ref[...] + beta_ref[...]
    o_ref[...] = _silu(v).astype(o_ref.dtype)


def _conformer(h, j, w_pw1, b_pw1, w_dw, b_dw, ln_g, ln_b, w_pw2, b_pw2, rows):
    m, d = h.shape
    nl = w_pw1.shape[0]
    tm, tn = _tile(rows.seq_s, 1024), _tile(d, 512)
    nt = d // tn
    b1 = b_pw1.reshape(nl, 1, 2 * d)
    glu = _fused_matmul(h, w_pw1, j, [0, nt], tm=tm, tn=tn, n_tiles=nt,
                        extras=[(b1, _bias_spec(j, tn)), (b1, _bias_spec(j, tn, nt))],
                        outs=[_out_tile(m, d, tm, tn, f32)], epilogue=_glu_epilogue, name="cv_pw1")[0]
    width = w_dw.shape[1]
    assert (width - 1) // 2 <= CONV_HALO
    rb = _tile(rows.seq_p, 256)
    assert rows.seq_p % rb == 0 and rows.seq_s % rb == 0 and rb % CONV_ROWS == 0 and d % CONV_LANES == 0
    hb = rb // CONV_HALO
    n_halo = m // CONV_HALO
    vec = lambda a: a.reshape(nl, 1, d)
    vspec = pl.BlockSpec((None, 1, d), lambda i: (j, 0, 0))
    mid = pl.pallas_call(
        functools.partial(_cv_mid_kernel, blocks_p=rows.mp // rb, per_seq_p=rows.seq_p // rb,
                          per_seq_s=rows.seq_s // rb),
        grid=(m // rb,),
        in_specs=[pl.BlockSpec((rb, d), lambda i: (i, 0)),
                  pl.BlockSpec((CONV_HALO, d), lambda i: (jnp.maximum(i * hb - 1, 0), 0)),
                  pl.BlockSpec((CONV_HALO, d), lambda i: (jnp.minimum((i + 1) * hb, n_halo - 1), 0)),
                  pl.BlockSpec((None, width, d), lambda i: (j, 0, 0)),
                  vspec, vspec, vspec],
        out_specs=pl.BlockSpec((rb, d), lambda i: (i, 0)),
        out_shape=jax.ShapeDtypeStruct((m, d), bf16),
        scratch_shapes=[pltpu.VMEM((rb + 2 * CONV_HALO, d), f32), pltpu.VMEM((rb, d), f32)],
        compiler_params=_params(1), name="cv_mid",
    )(glu, glu, glu, w_dw, vec(b_dw), vec(ln_g), vec(ln_b))
    tm2, tn2 = _tile(m, 1024), _tile(d, 1024)
    return _fused_matmul(mid, w_pw2, j, [0], tm=tm2, tn=tn2, n_tiles=d // tn2,
                         extras=[(vec(b_pw2), _bias_spec(j, tn2))],
                         outs=[_out_tile(m, d, tm2, tn2, f32)], epilogue=_epi_bias, name="cv_pw2")[0]


def _conv_silu_epilogue(accs, e_refs, o_refs, i, *, n_prompt_tiles, seq_p, seq_s):
    w_ref, b_ref = e_refs
    seq = jnp.where(i < n_prompt_tiles, seq_p, seq_s)
    u = _dwconv_rows(accs[0], w_ref[...], seq) + b_ref[...]
    o_refs[0][...] = _silu(u).astype(o_refs[0].dtype)


def _softplus_epilogue(accs, e_refs, o_refs, i):
    x = accs[0] + e_refs[0][...]
    o_refs[0][...] = jnp.maximum(x, 0.0) + jnp.log1p(jnp.exp(-jnp.abs(x)))


def _ssd_scan_kernel(xs_ref, bc_ref, dt_ref, alog_ref, e_ref, h0_ref, y_ref, st_ref,
                     state_ref, xc_ref, xcd_ref, sc_ref, cd_ref, acst_ref, acsc_ref,
                     *, nh, hp, ns, ng, tp, ncp, ncs):
    d = pl.program_id(0)
    t = pl.program_id(1)
    nt = pl.num_programs(1)
    lc = xs_ref.shape[0]
    r = nh // ng
    gp = r * hp
    fwd = d == 0
    te = jnp.where(fwd, t, nt - 1 - t)
    in_prompt = te < tp
    c_in_seq = jnp.where(in_prompt, te % ncp, (te - tp) % ncs)
    n_in_seq = jnp.where(in_prompt, ncp, ncs)
    is_first = c_in_seq == jnp.where(fwd, 0, n_in_seq - 1)
    is_last = c_in_seq == jnp.where(fwd, n_in_seq - 1, 0)

    @pl.when(jnp.logical_and(is_first, in_prompt))
    def _():
        state_ref[...] = jnp.zeros_like(state_ref)

    @pl.when(jnp.logical_and(is_first, jnp.logical_not(in_prompt)))
    def _():
        for g in range(ng):
            state_ref[:, g * gp:(g + 1) * gp] = h0_ref[g * r:(g + 1) * r].reshape(gp, ns).T

    dt2 = dt_ref[...]
    da2 = dt2 * (-jnp.exp(alog_ref[...]))
    row = lax.broadcasted_iota(jnp.int32, (lc, lc), 0)
    col = lax.broadcasted_iota(jnp.int32, (lc, lc), 1)
    diff = jnp.where(fwd, row - col, col - row)
    mask = diff >= 0
    tri = jnp.where(mask, 1.0, 0.0).astype(f32)
    acs2 = jnp.dot(tri, da2, precision=HIGHEST, preferred_element_type=f32)
    acst2 = acs2.T
    acs = jnp.where(fwd, acs2[:, :nh], acs2[:, nh:])
    acst = jnp.where(fwd, acst2[:nh], acst2[nh:])
    dt = jnp.where(fwd, dt2[:, :nh], dt2[:, nh:])
    total = jnp.where(fwd, acs[lc - 1:lc, :], acs[0:1, :])
    expand = e_ref[...]
    ex = lambda v: jnp.dot(v, expand, precision=HIGHEST, preferred_element_type=f32)
    xc = xs_ref[...] * ex(dt)
    xc_ref[...] = xc.astype(bf16)
    xcd_ref[...] = (xc * ex(jnp.exp(total - acs))).astype(bf16)
    sc_ref[...] = ex(jnp.exp(acs))
    cd_ref[...] = ex(jnp.broadcast_to(jnp.exp(total), (8, nh)))
    acst_ref[...] = acst
    for g in range(ng):
        acsc_ref[g] = acs[:, g * r:(g + 1) * r]

    def group(g, carry):
        lo = pl.multiple_of(g * gp, gp)
        bg = bc_ref[:, pl.ds(pl.multiple_of(g * ns, ns), ns)]
        cg = bc_ref[:, pl.ds(pl.multiple_of((ng + g) * ns, ns), ns)]
        scores = lax.dot_general(cg, bg, (((1,), (1,)), ((), ())), preferred_element_type=f32)
        st = state_ref[:, pl.ds(lo, gp)]
        y_off = jnp.dot(cg, st.astype(bf16), preferred_element_type=f32)
        cs = lax.dot_general(bg, xcd_ref[:, pl.ds(lo, gp)], (((0,), (0,)), ((), ())),
                             preferred_element_type=f32)
        state_ref[:, pl.ds(lo, gp)] = st * cd_ref[0:1, pl.ds(lo, gp)] + cs
        cols = acsc_ref[g]
        rws = acst_ref[pl.ds(pl.multiple_of(g * r, r), r), :]
        xcg = xc_ref[:, pl.ds(lo, gp)]
        ys = []
        for q in range(r):
            seg = cols[:, q:q + 1] - rws[q:q + 1, :]
            mh = (scores * jnp.where(mask, jnp.exp(seg), 0.0)).astype(bf16)
            ys.append(jnp.dot(mh, xcg[:, q * hp:(q + 1) * hp], preferred_element_type=f32))
        y = jnp.concatenate(ys, axis=1) + y_off * sc_ref[:, pl.ds(lo, gp)]
        y_ref[:, pl.ds(lo, gp)] = y.astype(y_ref.dtype)
        return carry

    lax.fori_loop(0, ng, group, 0)

    @pl.when(jnp.logical_and(is_last, in_prompt))
    def _():
        for g in range(ng):
            st_ref[g * r:(g + 1) * r] = state_ref[:, g * gp:(g + 1) * gp].T.reshape(r, hp, ns)


def _ssd_gate_kernel(y_ref, xs_ref, z_ref, dx_ref, g_ref, o_ref):
    y = y_ref[0].astype(f32) + y_ref[1].astype(f32) + dx_ref[...] * xs_ref[...]
    y = y * _silu(z_ref[...])
    o_ref[...] = _rms(y, g_ref[...]).astype(o_ref.dtype)


def _ssd_mixer(h, j, h0, w_in, w_conv, b_conv, dt_bias, a_log, d_skip, norm_g, w_out, rows, n_prompt):
    m, d = h.shape
    nl = w_in.shape[0]
    n_seq_s, _, _, nh, hp, ns = h0.shape
    di = nh * hp
    conv_dim = w_conv.shape[-1]
    ng = (conv_dim - di) // (2 * ns)
    lc = SSM_CHUNK
    assert rows.seq_p % lc == 0 and rows.seq_s % lc == 0 and nh % ng == 0
    tm = rows.seq_s
    npt = rows.mp // tm

    tnz = _tile(di, 1024)
    tmz = _tile(rows.seq_s, 1024)
    z = _fused_matmul(h, w_in, j, [0], tm=tmz, tn=tnz, n_tiles=di // tnz, extras=[],
                      outs=[_out_tile(m, di, tmz, tnz, f32)], epilogue=_epi_plain, name="ssd_in_z")[0]
    tnx = _tile(di, 256)
    assert di % tnx == 0 and (2 * ng * ns) % tnx == 0
    cw = w_conv.shape[1]
    bcv = b_conv.reshape(nl, 1, conv_dim)
    conv_epi = functools.partial(_conv_silu_epilogue, n_prompt_tiles=npt, seq_p=rows.seq_p, seq_s=rows.seq_s)

    def conv_part(col0, width_cols, dtype, name):
        nt = width_cols // tnx
        woff = (di + col0) // tnx
        coff = col0 // tnx
        return _fused_matmul(
            h, w_in, j, [woff], tm=tm, tn=tnx, n_tiles=nt,
            extras=[(w_conv, pl.BlockSpec((None, cw, tnx), lambda jj, i: (j, 0, coff + jj))),
                    (bcv, _bias_spec(j, tnx, coff))],
            outs=[_out_tile(m, width_cols, tm, tnx, dtype)], epilogue=conv_epi, name=name)[0]

    xs = conv_part(0, di, f32, "ssd_in_x")
    bc = conv_part(di, 2 * ng * ns, bf16, "ssd_in_bc")
    ndt = 2 * nh
    dt = _fused_matmul(h, w_in, j, [(di + conv_dim) // ndt], tm=tmz, tn=ndt, n_tiles=1,
                       extras=[(dt_bias.reshape(nl, 1, ndt), _bias_spec(j, ndt))],
                       outs=[_out_tile(m, ndt, tmz, ndt, f32)], epilogue=_softplus_epilogue, name="ssd_in_dt")[0]

    tp, ts = rows.mp // lc, rows.ms // lc
    ncp, ncs = rows.seq_p // lc, rows.seq_s // lc
    nt = tp + ts
    expand = jnp.repeat(jnp.eye(nh, dtype=f32), hp, axis=1)
    chunk = lambda dd, t: jnp.where(dd == 0, t, nt - 1 - t)
    rowmap = lambda dd, t: (chunk(dd, t), 0)
    y2, new_state = pl.pallas_call(
        functools.partial(_ssd_scan_kernel, nh=nh, hp=hp, ns=ns, ng=ng, tp=tp, ncp=ncp, ncs=ncs),
        grid=(2, nt),
        in_specs=[pl.BlockSpec((lc, di), rowmap),
                  pl.BlockSpec((lc, 2 * ng * ns), rowmap),
                  pl.BlockSpec((lc, ndt), rowmap),
                  pl.BlockSpec((None, 1, ndt), lambda dd, t: (j, 0, 0)),
                  pl.BlockSpec((nh, di), lambda dd, t: (0, 0)),
                  pl.BlockSpec((None, None, None, nh, hp, ns),
                               lambda dd, t: (jnp.clip((chunk(dd, t) - tp) // ncs, 0, n_seq_s - 1), j, dd, 0, 0, 0))],
        out_specs=[pl.BlockSpec((None, lc, di), lambda dd, t: (dd, chunk(dd, t), 0)),
                   pl.BlockSpec((None, None, nh, hp, ns),
                                lambda dd, t: (jnp.clip(chunk(dd, t) // ncp, 0, n_prompt - 1), dd, 0, 0, 0))],
        out_shape=[jax.ShapeDtypeStruct((2, m, di), bf16),
                   jax.ShapeDtypeStruct((n_prompt, 2, nh, hp, ns), f32)],
        scratch_shapes=[pltpu.VMEM((ns, di), f32), pltpu.VMEM((lc, di), bf16), pltpu.VMEM((lc, di), bf16),
                        pltpu.VMEM((lc, di), f32), pltpu.VMEM((8, di), f32),
                        pltpu.VMEM((nh, lc), f32), pltpu.VMEM((ng, lc, nh // ng), f32)],
        compiler_params=_params(2), name="ssd_scan",
    )(xs, bc, dt, a_log.reshape(nl, 1, ndt), expand, h0)

    tg = _tile(rows.seq_s, 256)
    vrow = pl.BlockSpec((tg, di), lambda i: (i, 0))
    gated = pl.pallas_call(
        _ssd_gate_kernel, grid=(m // tg,),
        in_specs=[pl.BlockSpec((2, tg, di), lambda i: (0, i, 0)), vrow, vrow,
                  pl.BlockSpec((None, 1, di), lambda i: (j, 0, 0)),
                  pl.BlockSpec((None, 1, di), lambda i: (j, 0, 0))],
        out_specs=vrow, out_shape=jax.ShapeDtypeStruct((m, di), bf16),
        compiler_params=_params(1), name="ssd_gate",
    )(y2, xs, z, jnp.repeat(d_skip, hp, axis=1).reshape(nl, 1, di), norm_g.reshape(nl, 1, di))
    tmo, tno = _tile(m, 512), _tile(d, 512)
    out = _fused_matmul(gated, w_out, j, [0], tm=tmo, tn=tno, n_tiles=d // tno, extras=[],
                        outs=[_out_tile(m, d, tmo, tno, f32)], epilogue=_epi_plain, name="ssd_out")[0]
    return out, new_state


def _rope(x, cos, sin):
    n = x.shape[-1]
    lane = lax.broadcasted_iota(jnp.int32, x.shape, x.ndim - 1)
    partner = jnp.where((lane & 1) == 0, pltpu.roll(x, n - 1, axis=x.ndim - 1), pltpu.roll(x, 1, axis=x.ndim - 1))
    return x * cos + partner * sin


def _q_epilogue(accs, e_refs, o_refs, i, *, hd, scale):
    gain_ref, cos_ref, sin_ref = e_refs
    acc = accs[0]
    cos, sin, gain = cos_ref[...], sin_ref[...], gain_ref[...]
    for hh in range(acc.shape[1] // hd):
        q = _rope(_rms(acc[:, hh * hd:(hh + 1) * hd], gain), cos, sin) * scale
        o_refs[0][:, hh * hd:(hh + 1) * hd] = q.astype(o_refs[0].dtype)


def _kv_epilogue(accs, e_refs, o_refs, i, *, hd, nkv):
    gain_ref, cos_ref, sin_ref = e_refs
    acc = accs[0]
    cos, sin, gain = cos_ref[...], sin_ref[...], gain_ref[...]
    o_refs[1][...] = acc
    for hh in range(nkv):
        kn = _rms(acc[:, hh * hd:(hh + 1) * hd], gain)
        o_refs[1][:, hh * hd:(hh + 1) * hd] = kn
        o_refs[0][:, hh * hd:(hh + 1) * hd] = _rope(kn, cos, sin).astype(o_refs[0].dtype)
    o_refs[0][:, nkv * hd:] = acc[:, nkv * hd:].astype(o_refs[0].dtype)


def _attn_kernel(q_ref, k_ref, v_ref, *rest, group, hd, cached):
    if cached:
        ck_ref, cv_ref, o_ref = rest
        ck = ck_ref[...].astype(bf16)
        cv = cv_ref[...].astype(bf16)
    else:
        (o_ref,) = rest
    k = k_ref[...]
    v = v_ref[...]
    nt = (((1,), (1,)), ((), ()))
    for g in range(group):
        q = q_ref[:, g * hd:(g + 1) * hd]
        s = lax.dot_general(q, k, nt, preferred_element_type=f32)
        mx = jnp.max(s, axis=-1, keepdims=True)
        if cached:
            s2 = lax.dot_general(q, ck, nt, preferred_element_type=f32)
            mx = jnp.maximum(mx, jnp.max(s2, axis=-1, keepdims=True))
        p = jnp.exp(s - mx)
        den = jnp.sum(p, axis=-1, keepdims=True)
        o = jnp.dot(p.astype(bf16), v, preferred_element_type=f32)
        if cached:
            p2 = jnp.exp(s2 - mx)
            den = den + jnp.sum(p2, axis=-1, keepdims=True)
            o = o + jnp.dot(p2.astype(bf16), cv, preferred_element_type=f32)
        o_ref[:, g * hd:(g + 1) * hd] = (o / den).astype(o_ref.dtype)


def _attention(h, j, cache_k, cache_v, w_qkv, q_norm, k_norm, w_o, rows, n_prompt):
    m, d = h.shape
    nl = w_qkv.shape[0]
    hd = q_norm.shape[-1]
    n_seq_s, _, past, nkv, _ = cache_k.shape
    nq = w_o.shape[1] // hd
    group = nq // nkv
    assert rows.seq_s % GRID_W == 0
    pos = jnp.arange(rows.seq_s)
    inv = ROPE_THETA ** (-jnp.arange(hd // 4, dtype=f32) / (hd // 4))
    ang = jnp.concatenate([(pos // GRID_W).astype(f32)[:, None] * inv, (pos % GRID_W).astype(f32)[:, None] * inv], -1)
    cos_s = jnp.repeat(jnp.cos(ang), 2, axis=-1)
    sin_s = jnp.stack([-jnp.sin(ang), jnp.sin(ang)], axis=-1).reshape(rows.seq_s, hd)
    reps = rows.ms // rows.seq_s
    cos_t = jnp.concatenate([jnp.ones((rows.mp, hd), f32), jnp.tile(cos_s, (reps, 1))], 0)
    sin_t = jnp.concatenate([jnp.zeros((rows.mp, hd), f32), jnp.tile(sin_s, (reps, 1))], 0)

    tm = _tile(rows.seq_s, 1024)
    tab = pl.BlockSpec((tm, hd), lambda jj, i: (i, 0))
    gspec = pl.BlockSpec((None, 1, hd), lambda jj, i: (j, 0, 0))
    nqc, nkc = nq * hd, nkv * hd
    tnq = _tile(nqc, 1024)
    q = _fused_matmul(h, w_qkv, j, [0], tm=tm, tn=tnq, n_tiles=nqc // tnq,
                      extras=[(q_norm.reshape(nl, 1, hd), gspec), (cos_t, tab), (sin_t, tab)],
                      outs=[_out_tile(m, nqc, tm, tnq, bf16)],
                      epilogue=functools.partial(_q_epilogue, hd=hd, scale=hd ** -0.5), name="attn_q")[0]
    tnk = 2 * nkc
    assert nqc % tnk == 0
    kv, kv_f32 = _fused_matmul(h, w_qkv, j, [nqc // tnk], tm=tm, tn=tnk, n_tiles=1,
                               extras=[(k_norm.reshape(nl, 1, hd), gspec), (cos_t, tab), (sin_t, tab)],
                               outs=[_out_tile(m, tnk, tm, tnk, bf16), _out_tile(m, tnk, tm, tnk, f32)],
                               epilogue=functools.partial(_kv_epilogue, hd=hd, nkv=nkv), name="attn_kv")

    gw = group * hd
    sp = rows.seq_p
    o_p = pl.pallas_call(
        functools.partial(_attn_kernel, group=group, hd=hd, cached=False),
        grid=(n_prompt, nkv),
        in_specs=[pl.BlockSpec((sp, gw), lambda b, kh: (b, kh)),
                  pl.BlockSpec((sp, hd), lambda b, kh: (b, kh)),
                  pl.BlockSpec((sp, hd), lambda b, kh: (b, nkv + kh))],
        out_specs=pl.BlockSpec((sp, gw), lambda b, kh: (b, kh)),
        out_shape=jax.ShapeDtypeStruct((rows.mp, nqc), bf16),
        compiler_params=_params(2), name="attn_prompt",
    )(q, kv, kv)
    ss = rows.seq_s
    tq = _tile(ss, 512)
    nqb = ss // tq
    pb = rows.mp // ss
    pbq = rows.mp // tq
    ck = cache_k[:, j].reshape(n_seq_s, past, nkc)
    cv = cache_v[:, j].reshape(n_seq_s, past, nkc)
    o_s = pl.pallas_call(
        functools.partial(_attn_kernel, group=group, hd=hd, cached=True),
        grid=(n_seq_s, nkv, nqb),
        in_specs=[pl.BlockSpec((tq, gw), lambda b, kh, qi: (pbq + b * nqb + qi, kh)),
                  pl.BlockSpec((ss, hd), lambda b, kh, qi: (pb + b, kh)),
                  pl.BlockSpec((ss, hd), lambda b, kh, qi: (pb + b, nkv + kh)),
                  pl.BlockSpec((None, past, hd), lambda b, kh, qi: (b, 0, kh)),
                  pl.BlockSpec((None, past, hd), lambda b, kh, qi: (b, 0, kh))],
        out_specs=pl.BlockSpec((tq, gw), lambda b, kh, qi: (b * nqb + qi, kh)),
        out_shape=jax.ShapeDtypeStruct((rows.ms, nqc), bf16),
        compiler_params=_params(3), name="attn_sample",
    )(q, kv, kv, ck, cv)
    o = jnp.concatenate([o_p, o_s], axis=0)
    tmo, tno = _tile(m, 1024), _tile(d, 1024)
    out = _fused_matmul(o, w_o, j, [0], tm=tmo, tn=tno, n_tiles=d // tno, extras=[],
                        outs=[_out_tile(m, d, tmo, tno, f32)], epilogue=_epi_plain, name="attn_o")[0]
    new_k = kv_f32[:rows.mp, :nkc].reshape(n_prompt, sp, nkv, hd)
    new_v = kv_f32[:rows.mp, nkc:].reshape(n_prompt, sp, nkv, hd)
    return out, new_k, new_v


def kernel(x_prompt, x_sample, c, state_ssd, cache_k, cache_v, c_ctx, w_mod, b_mod, norm_pre, norm_post, cv_w_pw1, cv_b_pw1, cv_w_dw, cv_b_dw, cv_ln_g, cv_ln_b, cv_w_pw2, cv_b_pw2, ssd_w_in, ssd_w_conv, ssd_b_conv, ssd_dt_bias, ssd_a_log, ssd_d, ssd_norm_g, ssd_w_out, attn_w_qkv, attn_q_norm, attn_k_norm, attn_w_o, ffn_w_up, ffn_w_dw, ffn_b_dw, ffn_w_down):
    nb, sp, d = x_prompt.shape
    ndb, ss, _ = x_sample.shape
    depth = w_mod.shape[0]
    rows = _Rows(nb * sp, ndb * ss, sp, ss)
    assert sp & (sp - 1) == 0 and ss & (ss - 1) == 0, "sequence lengths must be powers of two"
    assert rows.mp % ss == 0 and 1 + ndb <= COND_ROWS

    y = jnp.concatenate([x_prompt.reshape(rows.mp, d), x_sample.reshape(rows.ms, d)], axis=0)
    cond = jnp.concatenate([c_ctx[None, :], c, jnp.zeros((COND_ROWS - 1 - ndb, d), f32)], axis=0)
    mod = _modulation(cond, w_mod, b_mod).reshape(depth * COND_ROWS, 6, d)
    gpre = norm_pre.reshape(depth * 2, 1, d)
    gpost = norm_post.reshape(depth * 2, 1, d)
    h = _prenorm(y, gpre, 0, mod, 0, 1, 0, rows)
    new_ssd, new_k, new_v = [], [], []
    for i in range(depth):
        kind, j = i % N_MIXERS, i // N_MIXERS
        if kind == 0:
            o = _conformer(h, j, cv_w_pw1, cv_b_pw1, cv_w_dw, cv_b_dw, cv_ln_g, cv_ln_b, cv_w_pw2, cv_b_pw2, rows)
        elif kind == 1:
            o, st = _ssd_mixer(h, j, state_ssd, ssd_w_in, ssd_w_conv, ssd_b_conv, ssd_dt_bias, ssd_a_log,
                               ssd_d, ssd_norm_g, ssd_w_out, rows, nb)
            new_ssd.append(st)
        else:
            o, k_new, v_new = _attention(h, j, cache_k, cache_v, attn_w_qkv, attn_q_norm, attn_k_norm,
                                         attn_w_o, rows, nb)
            new_k.append(k_new)
            new_v.append(v_new)
        y, h = _resid(y, o, gpost, 2 * i, mod, i, 2, rows, nxt=(gpre, 2 * i + 1, i, 4, 3))
        o = _conv_ffn(h, i, ffn_w_up, ffn_w_dw, ffn_b_dw, ffn_w_down, rows)
        if i + 1 < depth:
            y, h = _resid(y, o, gpost, 2 * i + 1, mod, i, 5, rows, nxt=(gpre, 2 * i + 2, i + 1, 1, 0))
        else:
            y = _resid(y, o, gpost, 2 * i + 1, mod, i, 5, rows)
    yp = y[:rows.mp].reshape(nb, sp, d)
    ys = y[rows.mp:].reshape(ndb, ss, d)
    return (yp, ys, jnp.stack(new_ssd, axis=1), jnp.stack(new_k, axis=1), jnp.stack(new_v, axis=1))
```

```python
import functools
import math

import jax
import jax.numpy as jnp
from jax import lax
from jax.experimental import pallas as pl
from jax.experimental.pallas import tpu as pltpu

f32 = jnp.float32
bf16 = jnp.bfloat16
HIGHEST = lax.Precision.HIGHEST

EPS = 1e-6
GRID_W = 64
ROPE_THETA = 10000.0
SSM_CHUNK = 128
N_MIXERS = 3
V7X_VMEM_LIMIT_BYTES = 56 * 1024 * 1024
COND_ROWS = 8
ROW_HALO = 8
ROW_CHUNK = 256
SLAB_LANES = 128
EPILOGUE_LAG = 2


def _params(n_axes, vmem=V7X_VMEM_LIMIT_BYTES):
    return pltpu.CompilerParams(dimension_semantics=("arbitrary",) * n_axes,
                                vmem_limit_bytes=vmem)


def _tile(dim, pref):
    return pref if dim % pref == 0 else dim


def _sigmoid(x):
    return 1.0 / (1.0 + jnp.exp(-x))


def _silu(x):
    return x * _sigmoid(x)


def _rms(x, g):
    return x * lax.rsqrt(jnp.mean(x * x, axis=-1, keepdims=True) + EPS) * g


def _mod_kernel(c_ref, w_ref, b_ref, o_ref):
    s = _silu(c_ref[...]).astype(bf16)
    o_ref[...] = jnp.dot(s, w_ref[...].astype(bf16), preferred_element_type=f32) + b_ref[...]


def _modulation(cond, w_mod, b_mod):
    depth, d, n = w_mod.shape
    tn = _tile(n, 1024)
    return pl.pallas_call(
        _mod_kernel,
        grid=(depth, n // tn),
        in_specs=[pl.BlockSpec((COND_ROWS, d), lambda l, j: (0, 0)),
                  pl.BlockSpec((None, d, tn), lambda l, j: (l, 0, j)),
                  pl.BlockSpec((None, 1, tn), lambda l, j: (l, 0, j))],
        out_specs=pl.BlockSpec((None, COND_ROWS, tn), lambda l, j: (l, 0, j)),
        out_shape=jax.ShapeDtypeStruct((depth, COND_ROWS, n), f32),
        compiler_params=_params(2), name="modulation",
    )(cond, w_mod, b_mod.reshape(depth, 1, n))


def _prenorm_kernel(y_ref, g_ref, mod_ref, h_ref, *, sc, sh):
    yn = _rms(y_ref[...], g_ref[...])
    h_ref[...] = (yn * (1.0 + mod_ref[sc:sc + 1, :]) + mod_ref[sh:sh + 1, :]).astype(h_ref.dtype)


def _resid_kernel(y_ref, o_ref, gpost_ref, moda_ref, *rest, ga, nxt):
    y = y_ref[...] + moda_ref[ga:ga + 1, :] * _rms(o_ref[...], gpost_ref[...])
    if nxt is None:
        (ynew_ref,) = rest
        ynew_ref[...] = y
    else:
        gpre_ref, modb_ref, ynew_ref, h_ref = rest
        sc, sh = nxt
        ynew_ref[...] = y
        yn = _rms(y, gpre_ref[...])
        h_ref[...] = (yn * (1.0 + modb_ref[sc:sc + 1, :]) + modb_ref[sh:sh + 1, :]).astype(h_ref.dtype)


class _Rows:
    def __init__(self, mp, ms, seq_p, seq_s):
        self.mp, self.ms, self.m = mp, ms, mp + ms
        self.seq_p, self.seq_s = seq_p, seq_s
        self.chunk = _tile(seq_p, ROW_CHUNK)
        assert seq_p % self.chunk == 0 and seq_s % self.chunk == 0

    def cond_of_tile(self, tm):
        assert self.mp % tm == 0 and self.seq_s % tm == 0
        npt, per = self.mp // tm, self.seq_s // tm
        return lambda i: jnp.where(i < npt, 0, 1 + (i - npt) // per)


def _prenorm(y, g_all, g_idx, mod, layer, sc, sh, rows):
    m, d = y.shape
    tm = _tile(rows.seq_s, 256)
    cond = rows.cond_of_tile(tm)
    return pl.pallas_call(
        functools.partial(_prenorm_kernel, sc=sc, sh=sh),
        grid=(m // tm,),
        in_specs=[pl.BlockSpec((tm, d), lambda i: (i, 0)),
                  pl.BlockSpec((None, 1, d), lambda i: (g_idx, 0, 0)),
                  pl.BlockSpec((None, 6, d), lambda i: (layer * COND_ROWS + cond(i), 0, 0))],
        out_specs=pl.BlockSpec((tm, d), lambda i: (i, 0)),
        out_shape=jax.ShapeDtypeStruct((m, d), bf16),
        compiler_params=_params(1), name="prenorm",
    )(y, g_all, mod)


def _resid(y, o, gpost_all, gpost_idx, mod, layer, ga, rows, nxt=None):
    m, d = y.shape
    tm = _tile(rows.seq_s, 256)
    cond = rows.cond_of_tile(tm)
    row_spec = pl.BlockSpec((tm, d), lambda i: (i, 0))
    in_specs = [row_spec, row_spec,
                pl.BlockSpec((None, 1, d), lambda i: (gpost_idx, 0, 0)),
                pl.BlockSpec((None, 6, d), lambda i: (layer * COND_ROWS + cond(i), 0, 0))]
    args = [y, o, gpost_all, mod]
    if nxt is None:
        out_specs, out_shape, kn = row_spec, jax.ShapeDtypeStruct((m, d), f32), None
    else:
        gpre_all, gpre_idx, nlayer, sc, sh = nxt
        in_specs += [pl.BlockSpec((None, 1, d), lambda i: (gpre_idx, 0, 0)),
                     pl.BlockSpec((None, 6, d), lambda i: (nlayer * COND_ROWS + cond(i), 0, 0))]
        args += [gpre_all, mod]
        out_specs = [row_spec, row_spec]
        out_shape = [jax.ShapeDtypeStruct((m, d), f32), jax.ShapeDtypeStruct((m, d), bf16)]
        kn = (sc, sh)
    return pl.pallas_call(
        functools.partial(_resid_kernel, ga=ga, nxt=kn),
        grid=(m // tm,), in_specs=in_specs, out_specs=out_specs, out_shape=out_shape,
        compiler_params=_params(1), name="resid",
    )(*args)


def _fmm_kernel(*refs, n_slab, n_extra, n_out, epilogue, chunk):
    x_ref = refs[0]
    w_refs = refs[1:1 + n_slab]
    e_refs = refs[1 + n_slab:1 + n_slab + n_extra]
    o_refs = refs[1 + n_slab + n_extra:1 + n_slab + n_extra + n_out]
    wb_refs = refs[1 + n_slab + n_extra + n_out:1 + 2 * n_slab + n_extra + n_out]
    u_refs = refs[1 + 2 * n_slab + n_extra + n_out:]
    i = pl.program_id(1)

    @pl.when(i == 0)
    def _():
        for w_ref, wb_ref in zip(w_refs, wb_refs):
            wb_ref[...] = w_ref[...].astype(bf16)

    if chunk is None:
        x = x_ref[...]
        accs = [jnp.dot(x, wb_ref[...], preferred_element_type=f32) for wb_ref in wb_refs]
        epilogue(accs, e_refs, o_refs, i)
        return

    tm = x_ref.shape[0]
    for u_ref in u_refs:
        u_ref[0:ROW_HALO, :] = jnp.zeros((ROW_HALO, u_ref.shape[1]), f32)
        u_ref[ROW_HALO + tm:, :] = jnp.zeros((ROW_HALO, u_ref.shape[1]), f32)
    n_chunk = tm // chunk
    for c in range(n_chunk + EPILOGUE_LAG):
        if c < n_chunk:
            xc = x_ref[c * chunk:(c + 1) * chunk, :]
            for wb_ref, u_ref in zip(wb_refs, u_refs):
                u_ref[ROW_HALO + c * chunk:ROW_HALO + (c + 1) * chunk, :] = jnp.dot(
                    xc, wb_ref[...], preferred_element_type=f32)
        if c >= EPILOGUE_LAG:
            epilogue(c - EPILOGUE_LAG, u_refs, e_refs, o_refs, i)


def _fused_matmul(x, w_all, layer, slab_offs, *, tm, tn, n_tiles, extras, outs, epilogue, name, chunk=None):
    m, k = x.shape
    in_specs = [pl.BlockSpec((tm, k), lambda j, i: (i, 0))]
    args = [x]
    for off in slab_offs:
        in_specs.append(pl.BlockSpec((None, k, tn), lambda j, i, off=off: (layer, 0, off + j)))
        args.append(w_all)
    for arr, spec in extras:
        in_specs.append(spec)
        args.append(arr)
    scratch = [pltpu.VMEM((k, tn), bf16) for _ in slab_offs]
    if chunk is not None:
        assert tm % chunk == 0
        scratch += [pltpu.VMEM((tm + 2 * ROW_HALO, tn), f32) for _ in slab_offs]
    return pl.pallas_call(
        functools.partial(_fmm_kernel, n_slab=len(slab_offs), n_extra=len(extras),
                          n_out=len(outs), epilogue=epilogue, chunk=chunk),
        grid=(n_tiles, m // tm),
        in_specs=in_specs,
        out_specs=[spec for _, spec in outs],
        out_shape=[sds for sds, _ in outs],
        scratch_shapes=scratch,
        compiler_params=_params(2), name=name,
    )(*args)


def _bias_spec(layer, tn, off=0):
    return pl.BlockSpec((None, 1, tn), lambda j, i: (layer, 0, off + j))


def _out_tile(m, n, tm, tn, dtype):
    return (jax.ShapeDtypeStruct((m, n), dtype), pl.BlockSpec((tm, tn), lambda j, i: (i, j)))


def _epi_plain(accs, e_refs, o_refs, i):
    o_refs[0][...] = accs[0].astype(o_refs[0].dtype)


def _epi_bias(accs, e_refs, o_refs, i):
    o_refs[0][...] = (accs[0] + e_refs[0][...]).astype(o_refs[0].dtype)


def _dwconv_chunk(u_ref, w, c, chunk, seq):
    width = w.shape[0]
    left = (width - 1) // 2
    assert left <= ROW_HALO and width - 1 - left <= ROW_HALO and chunk >= 2 * ROW_HALO
    tn = u_ref.shape[1]
    base = ROW_HALO + c * chunk
    starts_seq = ((c * chunk) & (seq - 1)) == 0
    ends_seq = (((c + 1) * chunk) & (seq - 1)) == 0
    far = chunk + ROW_HALO

    def taps(r0, n_rows, masked):
        acc = None
        if masked:
            row = lax.broadcasted_iota(jnp.int32, (n_rows, tn), 0) + r0
        for k in range(width):
            d = k - left
            v = u_ref[pl.ds(base + r0 + d, n_rows), :]
            if masked and d < 0:
                v = jnp.where(row >= jnp.where(starts_seq, -d, -far), v, 0.0)
            if masked and d > 0:
                v = jnp.where(row < jnp.where(ends_seq, chunk - d, far), v, 0.0)
            term = v * w[k:k + 1, :]
            acc = term if acc is None else acc + term
        return acc

    body = taps(0, chunk, False)
    top = taps(0, ROW_HALO, True)
    bottom = taps(chunk - ROW_HALO, ROW_HALO, True)
    return jnp.concatenate([top, body[ROW_HALO:chunk - ROW_HALO], bottom], axis=0)


def _ffn_up_epilogue(c, u_refs, e_refs, o_refs, i, *, chunk, n_prompt_tiles, seq_p, seq_s):
    wa_ref, wg_ref, ba_ref, bg_ref = e_refs
    seq = jnp.where(i < n_prompt_tiles, seq_p, seq_s)
    a = _dwconv_chunk(u_refs[0], wa_ref[...], c, chunk, seq) + ba_ref[...]
    g = _dwconv_chunk(u_refs[1], wg_ref[...], c, chunk, seq) + bg_ref[...]
    o_refs[0][c * chunk:(c + 1) * chunk, :] = (_silu(g) * a).astype(o_refs[0].dtype)


def _conv_ffn(h, layer, w_up, w_dw, b_dw, w_down, rows):
    m, d = h.shape
    dff = w_down.shape[1]
    depth = w_up.shape[0]
    tm = rows.seq_s
    tn = _tile(dff, 256)
    nt = dff // tn
    width = w_dw.shape[1]
    b3 = b_dw.reshape(depth, 1, 2 * dff)
    dw_spec = lambda off: pl.BlockSpec((None, width, tn), lambda j, i: (layer, 0, off + j))
    act = _fused_matmul(
        h, w_up, layer, [0, nt], tm=tm, tn=tn, n_tiles=nt,
        extras=[(w_dw, dw_spec(0)), (w_dw, dw_spec(nt)), (b3, _bias_spec(layer, tn)), (b3, _bias_spec(layer, tn, nt))],
        outs=[_out_tile(m, dff, tm, tn, bf16)], chunk=rows.chunk,
        epilogue=functools.partial(_ffn_up_epilogue, chunk=rows.chunk, n_prompt_tiles=rows.mp // tm,
                                   seq_p=rows.seq_p, seq_s=rows.seq_s),
        name="ffn_up")[0]
    tm2, tn2 = _tile(m, 512), _tile(d, 512)
    return _fused_matmul(act, w_down, layer, [0], tm=tm2, tn=tn2, n_tiles=d // tn2, extras=[],
                         outs=[_out_tile(m, d, tm2, tn2, f32)], epilogue=_epi_plain, name="ffn_down")[0]


def _glu_epilogue(accs, e_refs, o_refs, i):
    a = accs[0] + e_refs[0][...]
    g = accs[1] + e_refs[1][...]
    o_refs[0][...] = (a * _sigmoid(g)).astype(o_refs[0].dtype)


CONV_HALO = 16
CONV_LANES = 128
CONV_ROWS = 128


def _cv_mid_kernel(u_ref, prev_ref, next_ref, w_ref, b_ref, g_ref, beta_ref, o_ref, pad_ref, conv_ref,
                   *, blocks_p, per_seq_p, per_seq_s):
    i = pl.program_id(0)
    rb, d = u_ref.shape
    width = w_ref.shape[0]
    left = (width - 1) // 2
    per = jnp.where(i < blocks_p, per_seq_p, per_seq_s)
    k_in_seq = jnp.where(i < blocks_p, i, i - blocks_p) % per
    first = k_in_seq == 0
    last = k_in_seq == per - 1
    pad_ref[0:CONV_HALO, :] = jnp.where(first, 0.0, prev_ref[...])
    pad_ref[CONV_HALO:CONV_HALO + rb, :] = u_ref[...]
    pad_ref[CONV_HALO + rb:CONV_HALO + rb + CONV_HALO, :] = jnp.where(last, 0.0, next_ref[...])

    def lane_chunk(cidx, carry):
        lo = pl.multiple_of(cidx * CONV_LANES, CONV_LANES)
        wc = w_ref[:, pl.ds(lo, CONV_LANES)]
        for r0 in range(0, rb, CONV_ROWS):
            acc = None
            for phase in range(ROW_HALO):
                ks = [k for k in range(width) if (CONV_HALO - left + k) % ROW_HALO == phase]
                if not ks:
                    continue
                span = max(CONV_HALO - left + k for k in ks) - phase
                shifted = pad_ref[pl.ds(r0 + phase, CONV_ROWS + span), pl.ds(lo, CONV_LANES)]
                for k in ks:
                    off = CONV_HALO - left + k - phase
                    term = shifted[off:off + CONV_ROWS] * wc[k:k + 1, :]
                    acc = term if acc is None else acc + term
            conv_ref[pl.ds(r0, CONV_ROWS), pl.ds(lo, CONV_LANES)] = acc
        return carry

    lax.fori_loop(0, d // CONV_LANES, lane_chunk, 0)
    u = conv_ref[...] + b_ref[...]
    mu = jnp.mean(u, axis=-1, keepdims=True)
    uc = u - mu
    v = uc * lax.rsqrt(jnp.mean(uc * uc, axis=-1, keepdims=True) + EPS) * g_ref[...] + beta_ref[...]
    o_ref[...] = _silu(v).astype(o_ref.dtype)


def _conformer(h, j, w_pw1, b_pw1, w_dw, b_dw, ln_g, ln_b, w_pw2, b_pw2, rows):
    m, d = h.shape
    nl = w_pw1.shape[0]
    tm, tn = _tile(rows.seq_s, 1024), _tile(d, 512)
    nt = d // tn
    b1 = b_pw1.reshape(nl, 1, 2 * d)
    glu = _fused_matmul(h, w_pw1, j, [0, nt], tm=tm, tn=tn, n_tiles=nt,
                        extras=[(b1, _bias_spec(j, tn)), (b1, _bias_spec(j, tn, nt))],
                        outs=[_out_tile(m, d, tm, tn, f32)], epilogue=_glu_epilogue, name="cv_pw1")[0]
    width = w_dw.shape[1]
    assert (width - 1) // 2 <= CONV_HALO
    rb = _tile(rows.seq_p, 256)
    assert rows.seq_p % rb == 0 and rows.seq_s % rb == 0 and rb % CONV_ROWS == 0 and d % CONV_LANES == 0
    hb = rb // CONV_HALO
    n_halo = m // CONV_HALO
    vec = lambda a: a.reshape(nl, 1, d)
    vspec = pl.BlockSpec((None, 1, d), lambda i: (j, 0, 0))
    mid = pl.pallas_call(
        functools.partial(_cv_mid_kernel, blocks_p=rows.mp // rb, per_seq_p=rows.seq_p // rb,
                          per_seq_s=rows.seq_s // rb),
        grid=(m // rb,),
        in_specs=[pl.BlockSpec((rb, d), lambda i: (i, 0)),
                  pl.BlockSpec((CONV_HALO, d), lambda i: (jnp.maximum(i * hb - 1, 0), 0)),
                  pl.BlockSpec((CONV_HALO, d), lambda i: (jnp.minimum((i + 1) * hb, n_halo - 1), 0)),
                  pl.BlockSpec((None, width, d), lambda i: (j, 0, 0)),
                  vspec, vspec, vspec],
        out_specs=pl.BlockSpec((rb, d), lambda i: (i, 0)),
        out_shape=jax.ShapeDtypeStruct((m, d), bf16),
        scratch_shapes=[pltpu.VMEM((rb + 2 * CONV_HALO, d), f32), pltpu.VMEM((rb, d), f32)],
        compiler_params=_params(1), name="cv_mid",
    )(glu, glu, glu, w_dw, vec(b_dw), vec(ln_g), vec(ln_b))
    tm2, tn2 = _tile(m, 1024), _tile(d, 1024)
    return _fused_matmul(mid, w_pw2, j, [0], tm=tm2, tn=tn2, n_tiles=d // tn2,
                         extras=[(vec(b_pw2), _bias_spec(j, tn2))],
                         outs=[_out_tile(m, d, tm2, tn2, f32)], epilogue=_epi_bias, name="cv_pw2")[0]


def _conv_silu_epilogue(c, u_refs, e_refs, o_refs, i, *, chunk, n_prompt_tiles, seq_p, seq_s):
    w_ref, b_ref = e_refs
    seq = jnp.where(i < n_prompt_tiles, seq_p, seq_s)
    u = _dwconv_chunk(u_refs[0], w_ref[...], c, chunk, seq) + b_ref[...]
    o_refs[0][c * chunk:(c + 1) * chunk, :] = _silu(u).astype(o_refs[0].dtype)


def _softplus_epilogue(accs, e_refs, o_refs, i):
    x = accs[0] + e_refs[0][...]
    o_refs[0][...] = jnp.maximum(x, 0.0) + jnp.log1p(jnp.exp(-jnp.abs(x)))


def _ssd_scan_kernel(xs_ref, bc_ref, dt_ref, alog_ref, h0_ref, y_ref, st_ref,
                     state_ref, acst_ref, dtt_ref, wt_ref, cdt_ref, acsc_ref,
                     *, nh, hp, ns, ng, tp, ncp, ncs):
    d = pl.program_id(0)
    t = pl.program_id(1)
    nt = pl.num_programs(1)
    lc = xs_ref.shape[0]
    r = nh // ng
    gp = r * hp
    fwd = d == 0
    te = jnp.where(fwd, t, nt - 1 - t)
    in_prompt = te < tp
    c_in_seq = jnp.where(in_prompt, te % ncp, (te - tp) % ncs)
    n_in_seq = jnp.where(in_prompt, ncp, ncs)
    is_first = c_in_seq == jnp.where(fwd, 0, n_in_seq - 1)
    is_last = c_in_seq == jnp.where(fwd, n_in_seq - 1, 0)

    @pl.when(jnp.logical_and(is_first, in_prompt))
    def _():
        state_ref[...] = jnp.zeros_like(state_ref)

    @pl.when(jnp.logical_and(is_first, jnp.logical_not(in_prompt)))
    def _():
        for g in range(ng):
            state_ref[:, g * gp:(g + 1) * gp] = h0_ref[g * r:(g + 1) * r].reshape(gp, ns).T

    dt2 = dt_ref[...]
    da2 = dt2 * (-jnp.exp(alog_ref[...]))
    row = lax.broadcasted_iota(jnp.int32, (lc, lc), 0)
    col = lax.broadcasted_iota(jnp.int32, (lc, lc), 1)
    diff = jnp.where(fwd, row - col, col - row)
    mask = diff >= 0
    tri = jnp.where(mask, 1.0, 0.0).astype(f32)
    acs2 = jnp.dot(tri, da2, precision=HIGHEST, preferred_element_type=f32)
    acst2 = acs2.T
    dtt2 = dt2.T
    acs = jnp.where(fwd, acs2[:, :nh], acs2[:, nh:])
    acst = jnp.where(fwd, acst2[:nh], acst2[nh:])
    dtt = jnp.where(fwd, dtt2[:nh], dtt2[nh:])
    total = jnp.where(fwd, acst[:, lc - 1:lc], acst[:, 0:1])
    totb = jnp.broadcast_to(total, (nh, lc))
    acst_ref[...] = acst
    dtt_ref[...] = dtt
    wt_ref[...] = dtt * jnp.exp(totb - acst)
    cdt_ref[...] = jnp.exp(totb)
    for g in range(ng):
        acsc_ref[g] = acs[:, g * r:(g + 1) * r]
    pair = SLAB_LANES // hp
    first_lanes = lax.broadcasted_iota(jnp.int32, (1, SLAB_LANES), 1) < hp

    def group(g, carry):
        lo = pl.multiple_of(g * gp, gp)
        bg = bc_ref[:, pl.ds(pl.multiple_of(g * ns, ns), ns)]
        cg = bc_ref[:, pl.ds(pl.multiple_of((ng + g) * ns, ns), ns)]
        scores = lax.dot_general(cg, bg, (((1,), (1,)), ((), ())), preferred_element_type=f32)
        bt = bg.astype(f32).T
        cf = cg.astype(f32)
        cols = acsc_ref[g]
        hrow = pl.ds(pl.multiple_of(g * r, r), r)
        rows_a, rows_dt, rows_w, rows_cd = acst_ref[hrow, :], dtt_ref[hrow, :], wt_ref[hrow, :], cdt_ref[hrow, :]
        zero = jnp.zeros((ns, ns), bf16)
        glanes = pl.ds(lo, gp)
        x_g = xs_ref[:, glanes].astype(bf16)
        st_g = state_ref[:, glanes]
        ys, sts = [], []
        for k in range(r // pair):
            st = st_g[:, k * SLAB_LANES:(k + 1) * SLAB_LANES]
            rhs = jnp.concatenate([x_g[:, k * SLAB_LANES:(k + 1) * SLAB_LANES], st.astype(bf16)], axis=0)
            blocks = []
            for q in (pair * k, pair * k + 1):
                a_l = jnp.broadcast_to(cols[:, q:q + 1], (lc, lc))
                seg = a_l - rows_a[q:q + 1, :]
                mh = scores * jnp.where(mask, jnp.exp(seg), 0.0) * rows_dt[q:q + 1, :]
                ch = cf * jnp.exp(a_l)
                bw = bt * rows_w[q:q + 1, :]
                blocks.append(jnp.concatenate([mh.astype(bf16), ch.astype(bf16)], axis=1))
                blocks.append(jnp.concatenate([bw.astype(bf16), zero], axis=1))
            out = jnp.dot(jnp.concatenate(blocks, axis=0), rhs, preferred_element_type=f32)
            h0_rows, h1_rows = out[:lc + ns], out[lc + ns:]
            y = jnp.where(first_lanes, h0_rows[:lc], h1_rows[:lc])
            cs = jnp.where(first_lanes, h0_rows[lc:], h1_rows[lc:])
            cd = jnp.where(first_lanes, rows_cd[pair * k:pair * k + 1, :], rows_cd[pair * k + 1:pair * k + 2, :])
            sts.append(st * cd + cs)
            ys.append(y.astype(y_ref.dtype))
        state_ref[:, glanes] = jnp.concatenate(sts, axis=1)
        y_ref[:, glanes] = jnp.concatenate(ys, axis=1)
        return carry

    lax.fori_loop(0, ng, group, 0, unroll=2)

    @pl.when(jnp.logical_and(is_last, in_prompt))
    def _():
        for g in range(ng):
            st_ref[g * r:(g + 1) * r] = state_ref[:, g * gp:(g + 1) * gp].T.reshape(r, hp, ns)


def _ssd_gate_kernel(y_ref, xs_ref, z_ref, dx_ref, g_ref, o_ref):
    y = y_ref[0].astype(f32) + y_ref[1].astype(f32) + dx_ref[...] * xs_ref[...]
    y = y * _silu(z_ref[...])
    o_ref[...] = _rms(y, g_ref[...]).astype(o_ref.dtype)


def _ssd_mixer(h, j, h0, w_in, w_conv, b_conv, dt_bias, a_log, d_skip, norm_g, w_out, rows, n_prompt):
    m, d = h.shape
    nl = w_in.shape[0]
    n_seq_s, _, _, nh, hp, ns = h0.shape
    di = nh * hp
    conv_dim = w_conv.shape[-1]
    ng = (conv_dim - di) // (2 * ns)
    lc = SSM_CHUNK
    assert rows.seq_p % lc == 0 and rows.seq_s % lc == 0 and nh % ng == 0
    tm = rows.seq_s
    npt = rows.mp // tm

    tnz = _tile(di, 1024)
    tmz = _tile(rows.seq_s, 1024)
    z = _fused_matmul(h, w_in, j, [0], tm=tmz, tn=tnz, n_tiles=di // tnz, extras=[],
                      outs=[_out_tile(m, di, tmz, tnz, f32)], epilogue=_epi_plain, name="ssd_in_z")[0]
    tnx = _tile(di, 512)
    assert di % tnx == 0 and (2 * ng * ns) % tnx == 0
    cw = w_conv.shape[1]
    bcv = b_conv.reshape(nl, 1, conv_dim)
    conv_epi = functools.partial(_conv_silu_epilogue, chunk=rows.chunk, n_prompt_tiles=npt,
                                 seq_p=rows.seq_p, seq_s=rows.seq_s)

    def conv_part(col0, width_cols, dtype, name):
        nt = width_cols // tnx
        woff = (di + col0) // tnx
        coff = col0 // tnx
        return _fused_matmul(
            h, w_in, j, [woff], tm=tm, tn=tnx, n_tiles=nt,
            extras=[(w_conv, pl.BlockSpec((None, cw, tnx), lambda jj, i: (j, 0, coff + jj))),
                    (bcv, _bias_spec(j, tnx, coff))],
            outs=[_out_tile(m, width_cols, tm, tnx, dtype)], epilogue=conv_epi, chunk=rows.chunk, name=name)[0]

    xs = conv_part(0, di, f32, "ssd_in_x")
    bc = conv_part(di, 2 * ng * ns, bf16, "ssd_in_bc")
    ndt = 2 * nh
    dt = _fused_matmul(h, w_in, j, [(di + conv_dim) // ndt], tm=tmz, tn=ndt, n_tiles=1,
                       extras=[(dt_bias.reshape(nl, 1, ndt), _bias_spec(j, ndt))],
                       outs=[_out_tile(m, ndt, tmz, ndt, f32)], epilogue=_softplus_epilogue, name="ssd_in_dt")[0]

    tp, ts = rows.mp // lc, rows.ms // lc
    ncp, ncs = rows.seq_p // lc, rows.seq_s // lc
    nt = tp + ts
    assert lc == SLAB_LANES and ns == lc and 2 * hp == SLAB_LANES and (nh // ng) % 2 == 0 and ng % 2 == 0
    chunk = lambda dd, t: jnp.where(dd == 0, t, nt - 1 - t)
    rowmap = lambda dd, t: (chunk(dd, t), 0)
    y2, new_state = pl.pallas_call(
        functools.partial(_ssd_scan_kernel, nh=nh, hp=hp, ns=ns, ng=ng, tp=tp, ncp=ncp, ncs=ncs),
        grid=(2, nt),
        in_specs=[pl.BlockSpec((lc, di), rowmap),
                  pl.BlockSpec((lc, 2 * ng * ns), rowmap),
                  pl.BlockSpec((lc, ndt), rowmap),
                  pl.BlockSpec((None, 1, ndt), lambda dd, t: (j, 0, 0)),
                  pl.BlockSpec((None, None, None, nh, hp, ns),
                               lambda dd, t: (jnp.clip((chunk(dd, t) - tp) // ncs, 0, n_seq_s - 1), j, dd, 0, 0, 0))],
        out_specs=[pl.BlockSpec((None, lc, di), lambda dd, t: (dd, chunk(dd, t), 0)),
                   pl.BlockSpec((None, None, nh, hp, ns),
                                lambda dd, t: (jnp.clip(chunk(dd, t) // ncp, 0, n_prompt - 1), dd, 0, 0, 0))],
        out_shape=[jax.ShapeDtypeStruct((2, m, di), bf16),
                   jax.ShapeDtypeStruct((n_prompt, 2, nh, hp, ns), f32)],
        scratch_shapes=[pltpu.VMEM((ns, di), f32)] + [pltpu.VMEM((nh, lc), f32)] * 4
                       + [pltpu.VMEM((ng, lc, nh // ng), f32)],
        compiler_params=_params(2), name="ssd_scan",
    )(xs, bc, dt, a_log.reshape(nl, 1, ndt), h0)

    tg = _tile(rows.seq_s, 256)
    vrow = pl.BlockSpec((tg, di), lambda i: (i, 0))
    gated = pl.pallas_call(
        _ssd_gate_kernel, grid=(m // tg,),
        in_specs=[pl.BlockSpec((2, tg, di), lambda i: (0, i, 0)), vrow, vrow,
                  pl.BlockSpec((None, 1, di), lambda i: (j, 0, 0)),
                  pl.BlockSpec((None, 1, di), lambda i: (j, 0, 0))],
        out_specs=vrow, out_shape=jax.ShapeDtypeStruct((m, di), bf16),
        compiler_params=_params(1), name="ssd_gate",
    )(y2, xs, z, jnp.repeat(d_skip, hp, axis=1).reshape(nl, 1, di), norm_g.reshape(nl, 1, di))
    tmo, tno = _tile(m, 512), _tile(d, 512)
    out = _fused_matmul(gated, w_out, j, [0], tm=tmo, tn=tno, n_tiles=d // tno, extras=[],
                        outs=[_out_tile(m, d, tmo, tno, f32)], epilogue=_epi_plain, name="ssd_out")[0]
    return out, new_state


def _rope(x, cos, sin):
    n = x.shape[-1]
    lane = lax.broadcasted_iota(jnp.int32, x.shape, x.ndim - 1)
    partner = jnp.where((lane & 1) == 0, pltpu.roll(x, n - 1, axis=x.ndim - 1), pltpu.roll(x, 1, axis=x.ndim - 1))
    return x * cos + partner * sin


def _q_epilogue(accs, e_refs, o_refs, i, *, hd, scale):
    gain_ref, cos_ref, sin_ref = e_refs
    acc = accs[0]
    cos, sin, gain = cos_ref[...], sin_ref[...], gain_ref[...]
    for hh in range(acc.shape[1] // hd):
        q = _rope(_rms(acc[:, hh * hd:(hh + 1) * hd], gain), cos, sin) * scale
        o_refs[0][:, hh * hd:(hh + 1) * hd] = q.astype(o_refs[0].dtype)


def _kv_epilogue(accs, e_refs, o_refs, i, *, hd, nkv):
    gain_ref, cos_ref, sin_ref = e_refs
    acc = accs[0]
    cos, sin, gain = cos_ref[...], sin_ref[...], gain_ref[...]
    o_refs[1][...] = acc
    for hh in range(nkv):
        kn = _rms(acc[:, hh * hd:(hh + 1) * hd], gain)
        o_refs[1][:, hh * hd:(hh + 1) * hd] = kn
        o_refs[0][:, hh * hd:(hh + 1) * hd] = _rope(kn, cos, sin).astype(o_refs[0].dtype)
    o_refs[0][:, nkv * hd:] = acc[:, nkv * hd:].astype(o_refs[0].dtype)


def _attn_kernel(q_ref, k_ref, v_ref, *rest, group, hd, cached):
    if cached:
        ck_ref, cv_ref, o_ref = rest
        ck = ck_ref[...].astype(bf16)
        cv = cv_ref[...].astype(bf16)
    else:
        (o_ref,) = rest
    k = k_ref[...]
    v = v_ref[...]
    nt = (((1,), (1,)), ((), ()))
    for g in range(group):
        q = q_ref[:, g * hd:(g + 1) * hd]
        s = lax.dot_general(q, k, nt, preferred_element_type=f32)
        mx = jnp.max(s, axis=-1, keepdims=True)
        if cached:
            s2 = lax.dot_general(q, ck, nt, preferred_element_type=f32)
            mx = jnp.maximum(mx, jnp.max(s2, axis=-1, keepdims=True))
        p = jnp.exp(s - mx)
        den = jnp.sum(p, axis=-1, keepdims=True)
        o = jnp.dot(p.astype(bf16), v, preferred_element_type=f32)
        if cached:
            p2 = jnp.exp(s2 - mx)
            den = den + jnp.sum(p2, axis=-1, keepdims=True)
            o = o + jnp.dot(p2.astype(bf16), cv, preferred_element_type=f32)
        o_ref[:, g * hd:(g + 1) * hd] = (o / den).astype(o_ref.dtype)


def _attention(h, j, cache_k, cache_v, w_qkv, q_norm, k_norm, w_o, rows, n_prompt):
    m, d = h.shape
    nl = w_qkv.shape[0]
    hd = q_norm.shape[-1]
    n_seq_s, _, past, nkv, _ = cache_k.shape
    nq = w_o.shape[1] // hd
    group = nq // nkv
    assert rows.seq_s % GRID_W == 0
    pos = jnp.arange(rows.seq_s)
    inv = ROPE_THETA ** (-jnp.arange(hd // 4, dtype=f32) / (hd // 4))
    ang = jnp.concatenate([(pos // GRID_W).astype(f32)[:, None] * inv, (pos % GRID_W).astype(f32)[:, None] * inv], -1)
    cos_s = jnp.repeat(jnp.cos(ang), 2, axis=-1)
    sin_s = jnp.stack([-jnp.sin(ang), jnp.sin(ang)], axis=-1).reshape(rows.seq_s, hd)
    reps = rows.ms // rows.seq_s
    cos_t = jnp.concatenate([jnp.ones((rows.mp, hd), f32), jnp.tile(cos_s, (reps, 1))], 0)
    sin_t = jnp.concatenate([jnp.zeros((rows.mp, hd), f32), jnp.tile(sin_s, (reps, 1))], 0)

    tm = _tile(rows.seq_s, 1024)
    tab = pl.BlockSpec((tm, hd), lambda jj, i: (i, 0))
    gspec = pl.BlockSpec((None, 1, hd), lambda jj, i: (j, 0, 0))
    nqc, nkc = nq * hd, nkv * hd
    tnq = _tile(nqc, 1024)
    q = _fused_matmul(h, w_qkv, j, [0], tm=tm, tn=tnq, n_tiles=nqc // tnq,
                      extras=[(q_norm.reshape(nl, 1, hd), gspec), (cos_t, tab), (sin_t, tab)],
                      outs=[_out_tile(m, nqc, tm, tnq, bf16)],
                      epilogue=functools.partial(_q_epilogue, hd=hd, scale=hd ** -0.5), name="attn_q")[0]
    tnk = 2 * nkc
    assert nqc % tnk == 0
    kv, kv_f32 = _fused_matmul(h, w_qkv, j, [nqc // tnk], tm=tm, tn=tnk, n_tiles=1,
                               extras=[(k_norm.reshape(nl, 1, hd), gspec), (cos_t, tab), (sin_t, tab)],
                               outs=[_out_tile(m, tnk, tm, tnk, bf16), _out_tile(m, tnk, tm, tnk, f32)],
                               epilogue=functools.partial(_kv_epilogue, hd=hd, nkv=nkv), name="attn_kv")

    gw = group * hd
    sp = rows.seq_p
    o_p = pl.pallas_call(
        functools.partial(_attn_kernel, group=group, hd=hd, cached=False),
        grid=(n_prompt, nkv),
        in_specs=[pl.BlockSpec((sp, gw), lambda b, kh: (b, kh)),
                  pl.BlockSpec((sp, hd), lambda b, kh: (b, kh)),
                  pl.BlockSpec((sp, hd), lambda b, kh: (b, nkv + kh))],
        out_specs=pl.BlockSpec((sp, gw), lambda b, kh: (b, kh)),
        out_shape=jax.ShapeDtypeStruct((rows.mp, nqc), bf16),
        compiler_params=_params(2), name="attn_prompt",
    )(q, kv, kv)
    ss = rows.seq_s
    tq = _tile(ss, 512)
    nqb = ss // tq
    pb = rows.mp // ss
    pbq = rows.mp // tq
    ck = cache_k[:, j].reshape(n_seq_s, past, nkc)
    cv = cache_v[:, j].reshape(n_seq_s, past, nkc)
    o_s = pl.pallas_call(
        functools.partial(_attn_kernel, group=group, hd=hd, cached=True),
        grid=(n_seq_s, nkv, nqb),
        in_specs=[pl.BlockSpec((tq, gw), lambda b, kh, qi: (pbq + b * nqb + qi, kh)),
                  pl.BlockSpec((ss, hd), lambda b, kh, qi: (pb + b, kh)),
                  pl.BlockSpec((ss, hd), lambda b, kh, qi: (pb + b, nkv + kh)),
                  pl.BlockSpec((None, past, hd), lambda b, kh, qi: (b, 0, kh)),
                  pl.BlockSpec((None, past, hd), lambda b, kh, qi: (b, 0, kh))],
        out_specs=pl.BlockSpec((tq, gw), lambda b, kh, qi: (b * nqb + qi, kh)),
        out_shape=jax.ShapeDtypeStruct((rows.ms, nqc), bf16),
        compiler_params=_params(3), name="attn_sample",
    )(q, kv, kv, ck, cv)
    o = jnp.concatenate([o_p, o_s], axis=0)
    tmo, tno = _tile(m, 1024), _tile(d, 1024)
    out = _fused_matmul(o, w_o, j, [0], tm=tmo, tn=tno, n_tiles=d // tno, extras=[],
                        outs=[_out_tile(m, d, tmo, tno, f32)], epilogue=_epi_plain, name="attn_o")[0]
    new_k = kv_f32[:rows.mp, :nkc].reshape(n_prompt, sp, nkv, hd)
    new_v = kv_f32[:rows.mp, nkc:].reshape(n_prompt, sp, nkv, hd)
    return out, new_k, new_v


def kernel(x_prompt, x_sample, c, state_ssd, cache_k, cache_v, c_ctx, w_mod, b_mod, norm_pre, norm_post, cv_w_pw1, cv_b_pw1, cv_w_dw, cv_b_dw, cv_ln_g, cv_ln_b, cv_w_pw2, cv_b_pw2, ssd_w_in, ssd_w_conv, ssd_b_conv, ssd_dt_bias, ssd_a_log, ssd_d, ssd_norm_g, ssd_w_out, attn_w_qkv, attn_q_norm, attn_k_norm, attn_w_o, ffn_w_up, ffn_w_dw, ffn_b_dw, ffn_w_down):
    nb, sp, d = x_prompt.shape
    ndb, ss, _ = x_sample.shape
    depth = w_mod.shape[0]
    rows = _Rows(nb * sp, ndb * ss, sp, ss)
    assert sp & (sp - 1) == 0 and ss & (ss - 1) == 0, "sequence lengths must be powers of two"
    assert rows.mp % ss == 0 and 1 + ndb <= COND_ROWS

    y = jnp.concatenate([x_prompt.reshape(rows.mp, d), x_sample.reshape(rows.ms, d)], axis=0)
    cond = jnp.concatenate([c_ctx[None, :], c, jnp.zeros((COND_ROWS - 1 - ndb, d), f32)], axis=0)
    mod = _modulation(cond, w_mod, b_mod).reshape(depth * COND_ROWS, 6, d)
    gpre = norm_pre.reshape(depth * 2, 1, d)
    gpost = norm_post.reshape(depth * 2, 1, d)
    h = _prenorm(y, gpre, 0, mod, 0, 1, 0, rows)
    new_ssd, new_k, new_v = [], [], []
    for i in range(depth):
        kind, j = i % N_MIXERS, i // N_MIXERS
        if kind == 0:
            o = _conformer(h, j, cv_w_pw1, cv_b_pw1, cv_w_dw, cv_b_dw, cv_ln_g, cv_ln_b, cv_w_pw2, cv_b_pw2, rows)
        elif kind == 1:
            o, st = _ssd_mixer(h, j, state_ssd, ssd_w_in, ssd_w_conv, ssd_b_conv, ssd_dt_bias, ssd_a_log,
                               ssd_d, ssd_norm_g, ssd_w_out, rows, nb)
            new_ssd.append(st)
        else:
            o, k_new, v_new = _attention(h, j, cache_k, cache_v, attn_w_qkv, attn_q_norm, attn_k_norm,
                                         attn_w_o, rows, nb)
            new_k.append(k_new)
            new_v.append(v_new)
        y, h = _resid(y, o, gpost, 2 * i, mod, i, 2, rows, nxt=(gpre, 2 * i + 1, i, 4, 3))
        o = _conv_ffn(h, i, ffn_w_up, ffn_w_dw, ffn_b_dw, ffn_w_down, rows)
        if i + 1 < depth:
            y, h = _resid(y, o, gpost, 2 * i + 1, mod, i, 5, rows, nxt=(gpre, 2 * i + 2, i + 1, 1, 0))
        else:
            y = _resid(y, o, gpost, 2 * i + 1, mod, i, 5, rows)
    yp = y[:rows.mp].reshape(nb, sp, d)
    ys = y[rows.mp:].reshape(ndb, ss, d)
    return (yp, ys, jnp.stack(new_ssd, axis=1), jnp.stack(new_k, axis=1), jnp.stack(new_v, axis=1))
```

```python
import functools
import math

import jax
import jax.numpy as jnp
from jax import lax
from jax.experimental import pallas as pl
from jax.experimental.pallas import tpu as pltpu

f32 = jnp.float32
bf16 = jnp.bfloat16
HIGHEST = lax.Precision.HIGHEST

EPS = 1e-6
GRID_W = 64
ROPE_THETA = 10000.0
SSM_CHUNK = 128
N_MIXERS = 3
V7X_VMEM_LIMIT_BYTES = 56 * 1024 * 1024
COND_ROWS = 8
ROW_HALO = 8
ROW_CHUNK = 256
SLAB_LANES = 128
EPILOGUE_LAG = 2


def _params(n_axes, vmem=V7X_VMEM_LIMIT_BYTES):
    return pltpu.CompilerParams(dimension_semantics=("arbitrary",) * n_axes,
                                vmem_limit_bytes=vmem)


def _tile(dim, pref):
    return pref if dim % pref == 0 else dim


def _sigmoid(x):
    return 1.0 / (1.0 + jnp.exp(-x))


def _silu(x):
    return x * _sigmoid(x)


def _rms(x, g):
    return x * lax.rsqrt(jnp.mean(x * x, axis=-1, keepdims=True) + EPS) * g


def _mod_kernel(c_ref, w_ref, b_ref, o_ref):
    s = _silu(c_ref[...]).astype(bf16)
    o_ref[...] = jnp.dot(s, w_ref[...].astype(bf16), preferred_element_type=f32) + b_ref[...]


def _modulation(cond, w_mod, b_mod):
    depth, d, n = w_mod.shape
    tn = _tile(n, 1024)
    return pl.pallas_call(
        _mod_kernel,
        grid=(depth, n // tn),
        in_specs=[pl.BlockSpec((COND_ROWS, d), lambda l, j: (0, 0)),
                  pl.BlockSpec((None, d, tn), lambda l, j: (l, 0, j)),
                  pl.BlockSpec((None, 1, tn), lambda l, j: (l, 0, j))],
        out_specs=pl.BlockSpec((None, COND_ROWS, tn), lambda l, j: (l, 0, j)),
        out_shape=jax.ShapeDtypeStruct((depth, COND_ROWS, n), f32),
        compiler_params=_params(2), name="modulation",
    )(cond, w_mod, b_mod.reshape(depth, 1, n))


def _prenorm_kernel(y_ref, g_ref, mod_ref, h_ref, *, sc, sh):
    yn = _rms(y_ref[...], g_ref[...])
    h_ref[...] = (yn * (1.0 + mod_ref[sc:sc + 1, :]) + mod_ref[sh:sh + 1, :]).astype(h_ref.dtype)


def _resid_kernel(y_ref, o_ref, gpost_ref, moda_ref, *rest, ga, nxt):
    y = y_ref[...] + moda_ref[ga:ga + 1, :] * _rms(o_ref[...], gpost_ref[...])
    if nxt is None:
        (ynew_ref,) = rest
        ynew_ref[...] = y
    else:
        gpre_ref, modb_ref, ynew_ref, h_ref = rest
        sc, sh = nxt
        ynew_ref[...] = y
        yn = _rms(y, gpre_ref[...])
        h_ref[...] = (yn * (1.0 + modb_ref[sc:sc + 1, :]) + modb_ref[sh:sh + 1, :]).astype(h_ref.dtype)


class _Rows:
    def __init__(self, mp, ms, seq_p, seq_s):
        self.mp, self.ms, self.m = mp, ms, mp + ms
        self.seq_p, self.seq_s = seq_p, seq_s
        self.chunk = _tile(seq_p, ROW_CHUNK)
        assert seq_p % self.chunk == 0 and seq_s % self.chunk == 0

    def cond_of_tile(self, tm):
        assert self.mp % tm == 0 and self.seq_s % tm == 0
        npt, per = self.mp // tm, self.seq_s // tm
        return lambda i: jnp.where(i < npt, 0, 1 + (i - npt) // per)


def _prenorm(y, g_all, g_idx, mod, layer, sc, sh, rows):
    m, d = y.shape
    tm = _tile(rows.seq_s, 256)
    cond = rows.cond_of_tile(tm)
    return pl.pallas_call(
        functools.partial(_prenorm_kernel, sc=sc, sh=sh),
        grid=(m // tm,),
        in_specs=[pl.BlockSpec((tm, d), lambda i: (i, 0)),
                  pl.BlockSpec((None, 1, d), lambda i: (g_idx, 0, 0)),
                  pl.BlockSpec((None, 6, d), lambda i: (layer * COND_ROWS + cond(i), 0, 0))],
        out_specs=pl.BlockSpec((tm, d), lambda i: (i, 0)),
        out_shape=jax.ShapeDtypeStruct((m, d), bf16),
        compiler_params=_params(1), name="prenorm",
    )(y, g_all, mod)


def _resid(y, o, gpost_all, gpost_idx, mod, layer, ga, rows, nxt=None):
    m, d = y.shape
    tm = _tile(rows.seq_s, 256)
    cond = rows.cond_of_tile(tm)
    row_spec = pl.BlockSpec((tm, d), lambda i: (i, 0))
    in_specs = [row_spec, row_spec,
                pl.BlockSpec((None, 1, d), lambda i: (gpost_idx, 0, 0)),
                pl.BlockSpec((None, 6, d), lambda i: (layer * COND_ROWS + cond(i), 0, 0))]
    args = [y, o, gpost_all, mod]
    if nxt is None:
        out_specs, out_shape, kn = row_spec, jax.ShapeDtypeStruct((m, d), f32), None
    else:
        gpre_all, gpre_idx, nlayer, sc, sh = nxt
        in_specs += [pl.BlockSpec((None, 1, d), lambda i: (gpre_idx, 0, 0)),
                     pl.BlockSpec((None, 6, d), lambda i: (nlayer * COND_ROWS + cond(i), 0, 0))]
        args += [gpre_all, mod]
        out_specs = [row_spec, row_spec]
        out_shape = [jax.ShapeDtypeStruct((m, d), f32), jax.ShapeDtypeStruct((m, d), bf16)]
        kn = (sc, sh)
    return pl.pallas_call(
        functools.partial(_resid_kernel, ga=ga, nxt=kn),
        grid=(m // tm,), in_specs=in_specs, out_specs=out_specs, out_shape=out_shape,
        compiler_params=_params(1), name="resid",
    )(*args)


def _fmm_kernel(*refs, n_slab, n_extra, n_out, epilogue, chunk):
    x_ref = refs[0]
    w_refs = refs[1:1 + n_slab]
    e_refs = refs[1 + n_slab:1 + n_slab + n_extra]
    o_refs = refs[1 + n_slab + n_extra:1 + n_slab + n_extra + n_out]
    wb_refs = refs[1 + n_slab + n_extra + n_out:1 + 2 * n_slab + n_extra + n_out]
    u_refs = refs[1 + 2 * n_slab + n_extra + n_out:]
    i = pl.program_id(1)

    @pl.when(i == 0)
    def _():
        for w_ref, wb_ref in zip(w_refs, wb_refs):
            wb_ref[...] = w_ref[...].astype(bf16)

    if chunk is None:
        x = x_ref[...]
        accs = [jnp.dot(x, wb_ref[...], preferred_element_type=f32) for wb_ref in wb_refs]
        epilogue(accs, e_refs, o_refs, i)
        return

    tm = x_ref.shape[0]
    n_chunk = tm // chunk
    per_slab = [u_refs[s * n_chunk:(s + 1) * n_chunk] for s in range(n_slab)]
    zeros = jnp.zeros((ROW_HALO, u_refs[0].shape[1]), f32)
    for bufs in per_slab:
        bufs[0][0:ROW_HALO, :] = zeros
        bufs[-1][ROW_HALO + chunk:, :] = zeros
    for c in range(n_chunk + EPILOGUE_LAG):
        if c < n_chunk:
            xc = x_ref[c * chunk:(c + 1) * chunk, :]
            for wb_ref, bufs in zip(wb_refs, per_slab):
                prod = jnp.dot(xc, wb_ref[...], preferred_element_type=f32)
                bufs[c][ROW_HALO:ROW_HALO + chunk, :] = prod
                if c > 0:
                    bufs[c - 1][ROW_HALO + chunk:, :] = prod[0:ROW_HALO]
                if c + 1 < n_chunk:
                    bufs[c + 1][0:ROW_HALO, :] = prod[chunk - ROW_HALO:]
        if c >= EPILOGUE_LAG:
            done = c - EPILOGUE_LAG
            epilogue(done, [bufs[done] for bufs in per_slab], e_refs, o_refs, i)


def _fused_matmul(x, w_all, layer, slab_offs, *, tm, tn, n_tiles, extras, outs, epilogue, name, chunk=None):
    m, k = x.shape
    in_specs = [pl.BlockSpec((tm, k), lambda j, i: (i, 0))]
    args = [x]
    for off in slab_offs:
        in_specs.append(pl.BlockSpec((None, k, tn), lambda j, i, off=off: (layer, 0, off + j)))
        args.append(w_all)
    for arr, spec in extras:
        in_specs.append(spec)
        args.append(arr)
    scratch = [pltpu.VMEM((k, tn), bf16) for _ in slab_offs]
    if chunk is not None:
        assert tm % chunk == 0
        scratch += [pltpu.VMEM((chunk + 2 * ROW_HALO, tn), f32)] * (len(slab_offs) * (tm // chunk))
    return pl.pallas_call(
        functools.partial(_fmm_kernel, n_slab=len(slab_offs), n_extra=len(extras),
                          n_out=len(outs), epilogue=epilogue, chunk=chunk),
        grid=(n_tiles, m // tm),
        in_specs=in_specs,
        out_specs=[spec for _, spec in outs],
        out_shape=[sds for sds, _ in outs],
        scratch_shapes=scratch,
        compiler_params=_params(2), name=name,
    )(*args)


def _bias_spec(layer, tn, off=0):
    return pl.BlockSpec((None, 1, tn), lambda j, i: (layer, 0, off + j))


def _out_tile(m, n, tm, tn, dtype):
    return (jax.ShapeDtypeStruct((m, n), dtype), pl.BlockSpec((tm, tn), lambda j, i: (i, j)))


def _epi_plain(accs, e_refs, o_refs, i):
    o_refs[0][...] = accs[0].astype(o_refs[0].dtype)


def _epi_bias(accs, e_refs, o_refs, i):
    o_refs[0][...] = (accs[0] + e_refs[0][...]).astype(o_refs[0].dtype)


def _dwconv_chunk(u_ref, w, c, chunk, seq):
    width = w.shape[0]
    left = (width - 1) // 2
    assert left <= ROW_HALO and width - 1 - left <= ROW_HALO and chunk >= 2 * ROW_HALO
    tn = u_ref.shape[1]
    base = ROW_HALO
    starts_seq = ((c * chunk) & (seq - 1)) == 0
    ends_seq = (((c + 1) * chunk) & (seq - 1)) == 0
    far = chunk + ROW_HALO

    def taps(r0, n_rows, masked):
        acc = None
        if masked:
            row = lax.broadcasted_iota(jnp.int32, (n_rows, tn), 0) + r0
        for k in range(width):
            d = k - left
            v = u_ref[pl.ds(base + r0 + d, n_rows), :]
            if masked and d < 0:
                v = jnp.where(row >= jnp.where(starts_seq, -d, -far), v, 0.0)
            if masked and d > 0:
                v = jnp.where(row < jnp.where(ends_seq, chunk - d, far), v, 0.0)
            term = v * w[k:k + 1, :]
            acc = term if acc is None else acc + term
        return acc

    body = taps(0, chunk, False)
    top = taps(0, ROW_HALO, True)
    bottom = taps(chunk - ROW_HALO, ROW_HALO, True)
    return jnp.concatenate([top, body[ROW_HALO:chunk - ROW_HALO], bottom], axis=0)


def _ffn_up_epilogue(c, u_refs, e_refs, o_refs, i, *, chunk, n_prompt_tiles, seq_p, seq_s):
    wa_ref, wg_ref, ba_ref, bg_ref = e_refs
    seq = jnp.where(i < n_prompt_tiles, seq_p, seq_s)
    a = _dwconv_chunk(u_refs[0], wa_ref[...], c, chunk, seq) + ba_ref[...]
    g = _dwconv_chunk(u_refs[1], wg_ref[...], c, chunk, seq) + bg_ref[...]
    o_refs[0][c * chunk:(c + 1) * chunk, :] = (_silu(g) * a).astype(o_refs[0].dtype)


def _conv_ffn(h, layer, w_up, w_dw, b_dw, w_down, rows):
    m, d = h.shape
    dff = w_down.shape[1]
    depth = w_up.shape[0]
    tm = rows.seq_s
    tn = _tile(dff, 512)
    nt = dff // tn
    width = w_dw.shape[1]
    b3 = b_dw.reshape(depth, 1, 2 * dff)
    dw_spec = lambda off: pl.BlockSpec((None, width, tn), lambda j, i: (layer, 0, off + j))
    act = _fused_matmul(
        h, w_up, layer, [0, nt], tm=tm, tn=tn, n_tiles=nt,
        extras=[(w_dw, dw_spec(0)), (w_dw, dw_spec(nt)), (b3, _bias_spec(layer, tn)), (b3, _bias_spec(layer, tn, nt))],
        outs=[_out_tile(m, dff, tm, tn, bf16)], chunk=rows.chunk,
        epilogue=functools.partial(_ffn_up_epilogue, chunk=rows.chunk, n_prompt_tiles=rows.mp // tm,
                                   seq_p=rows.seq_p, seq_s=rows.seq_s),
        name="ffn_up")[0]
    tm2, tn2 = _tile(m, 512), _tile(d, 512)
    return _fused_matmul(act, w_down, layer, [0], tm=tm2, tn=tn2, n_tiles=d // tn2, extras=[],
                         outs=[_out_tile(m, d, tm2, tn2, f32)], epilogue=_epi_plain, name="ffn_down")[0]


def _glu_epilogue(accs, e_refs, o_refs, i):
    a = accs[0] + e_refs[0][...]
    g = accs[1] + e_refs[1][...]
    o_refs[0][...] = (a * _sigmoid(g)).astype(o_refs[0].dtype)


CONV_HALO = 16
CONV_LANES = 128
CONV_ROWS = 128
LN_ROWS = 32


def _cv_mid_kernel(u_ref, prev_ref, next_ref, w_ref, b_ref, g_ref, beta_ref, o_ref, pad_ref, conv_ref,
                   *, blocks_p, per_seq_p, per_seq_s):
    i = pl.program_id(0)
    rb, d = u_ref.shape
    width = w_ref.shape[0]
    left = (width - 1) // 2
    per = jnp.where(i < blocks_p, per_seq_p, per_seq_s)
    k_in_seq = jnp.where(i < blocks_p, i, i - blocks_p) % per
    first = k_in_seq == 0
    last = k_in_seq == per - 1
    pad_ref[0:CONV_HALO, :] = jnp.where(first, 0.0, prev_ref[...])
    pad_ref[CONV_HALO:CONV_HALO + rb, :] = u_ref[...]
    pad_ref[CONV_HALO + rb:CONV_HALO + rb + CONV_HALO, :] = jnp.where(last, 0.0, next_ref[...])

    def lane_chunk(cidx, carry):
        lo = pl.multiple_of(cidx * CONV_LANES, CONV_LANES)
        wc = w_ref[:, pl.ds(lo, CONV_LANES)]
        for r0 in range(0, rb, CONV_ROWS):
            partial = []
            for phase in range(ROW_HALO):
                ks = [k for k in range(width) if (CONV_HALO - left + k) % ROW_HALO == phase]
                if not ks:
                    continue
                span = max(CONV_HALO - left + k for k in ks) - phase
                shifted = pad_ref[pl.ds(r0 + phase, CONV_ROWS + span), pl.ds(lo, CONV_LANES)]
                acc = None
                for k in ks:
                    off = CONV_HALO - left + k - phase
                    term = shifted[off:off + CONV_ROWS] * wc[k:k + 1, :]
                    acc = term if acc is None else acc + term
                partial.append(acc)
            total = partial[0]
            for acc in partial[1:]:
                total = total + acc
            conv_ref[pl.ds(r0, CONV_ROWS), pl.ds(lo, CONV_LANES)] = total
        return carry

    lax.fori_loop(0, d // CONV_LANES, lane_chunk, 0)

    def norm_rows(ridx, carry):
        rs = pl.ds(pl.multiple_of(ridx * LN_ROWS, LN_ROWS), LN_ROWS)
        u = conv_ref[rs, :] + b_ref[...]
        mu = jnp.mean(u, axis=-1, keepdims=True)
        uc = u - mu
        v = uc * lax.rsqrt(jnp.mean(uc * uc, axis=-1, keepdims=True) + EPS) * g_ref[...] + beta_ref[...]
        o_ref[rs, :] = _silu(v).astype(o_ref.dtype)
        return carry

    lax.fori_loop(0, rb // LN_ROWS, norm_rows, 0)


def _conformer(h, j, w_pw1, b_pw1, w_dw, b_dw, ln_g, ln_b, w_pw2, b_pw2, rows):
    m, d = h.shape
    nl = w_pw1.shape[0]
    tm, tn = _tile(rows.seq_s, 1024), _tile(d, 512)
    nt = d // tn
    b1 = b_pw1.reshape(nl, 1, 2 * d)
    glu = _fused_matmul(h, w_pw1, j, [0, nt], tm=tm, tn=tn, n_tiles=nt,
                        extras=[(b1, _bias_spec(j, tn)), (b1, _bias_spec(j, tn, nt))],
                        outs=[_out_tile(m, d, tm, tn, f32)], epilogue=_glu_epilogue, name="cv_pw1")[0]
    width = w_dw.shape[1]
    assert (width - 1) // 2 <= CONV_HALO
    rb = _tile(rows.seq_p, 256)
    assert rows.seq_p % rb == 0 and rows.seq_s % rb == 0 and rb % CONV_ROWS == 0 and d % CONV_LANES == 0
    hb = rb // CONV_HALO
    n_halo = m // CONV_HALO
    vec = lambda a: a.reshape(nl, 1, d)
    vspec = pl.BlockSpec((None, 1, d), lambda i: (j, 0, 0))
    mid = pl.pallas_call(
        functools.partial(_cv_mid_kernel, blocks_p=rows.mp // rb, per_seq_p=rows.seq_p // rb,
                          per_seq_s=rows.seq_s // rb),
        grid=(m // rb,),
        in_specs=[pl.BlockSpec((rb, d), lambda i: (i, 0)),
                  pl.BlockSpec((CONV_HALO, d), lambda i: (jnp.maximum(i * hb - 1, 0), 0)),
                  pl.BlockSpec((CONV_HALO, d), lambda i: (jnp.minimum((i + 1) * hb, n_halo - 1), 0)),
                  pl.BlockSpec((None, width, d), lambda i: (j, 0, 0)),
                  vspec, vspec, vspec],
        out_specs=pl.BlockSpec((rb, d), lambda i: (i, 0)),
        out_shape=jax.ShapeDtypeStruct((m, d), bf16),
        scratch_shapes=[pltpu.VMEM((rb + 2 * CONV_HALO, d), f32), pltpu.VMEM((rb, d), f32)],
        compiler_params=_params(1), name="cv_mid",
    )(glu, glu, glu, w_dw, vec(b_dw), vec(ln_g), vec(ln_b))
    tm2, tn2 = _tile(m, 1024), _tile(d, 1024)
    return _fused_matmul(mid, w_pw2, j, [0], tm=tm2, tn=tn2, n_tiles=d // tn2,
                         extras=[(vec(b_pw2), _bias_spec(j, tn2))],
                         outs=[_out_tile(m, d, tm2, tn2, f32)], epilogue=_epi_bias, name="cv_pw2")[0]


def _conv_silu_epilogue(c, u_refs, e_refs, o_refs, i, *, chunk, n_prompt_tiles, seq_p, seq_s):
    w_ref, b_ref = e_refs
    seq = jnp.where(i < n_prompt_tiles, seq_p, seq_s)
    u = _dwconv_chunk(u_refs[0], w_ref[...], c, chunk, seq) + b_ref[...]
    o_refs[0][c * chunk:(c + 1) * chunk, :] = _silu(u).astype(o_refs[0].dtype)


def _softplus_epilogue(accs, e_refs, o_refs, i):
    x = accs[0] + e_refs[0][...]
    o_refs[0][...] = jnp.maximum(x, 0.0) + jnp.log1p(jnp.exp(-jnp.abs(x)))


def _ssd_scan_kernel(xs_ref, bc_ref, dt_ref, alog_ref, h0_ref, y_ref, st_ref,
                     state_ref, acst_ref, dtt_ref, wt_ref, cdt_ref, acsc_ref,
                     *, nh, hp, ns, ng, tp, ncp, ncs):
    d = pl.program_id(0)
    t = pl.program_id(1)
    nt = pl.num_programs(1)
    lc = xs_ref.shape[0]
    r = nh // ng
    gp = r * hp
    fwd = d == 0
    te = jnp.where(fwd, t, nt - 1 - t)
    in_prompt = te < tp
    c_in_seq = jnp.where(in_prompt, te % ncp, (te - tp) % ncs)
    n_in_seq = jnp.where(in_prompt, ncp, ncs)
    is_first = c_in_seq == jnp.where(fwd, 0, n_in_seq - 1)
    is_last = c_in_seq == jnp.where(fwd, n_in_seq - 1, 0)

    @pl.when(jnp.logical_and(is_first, in_prompt))
    def _():
        state_ref[...] = jnp.zeros_like(state_ref)

    @pl.when(jnp.logical_and(is_first, jnp.logical_not(in_prompt)))
    def _():
        for g in range(ng):
            state_ref[:, g * gp:(g + 1) * gp] = h0_ref[g * r:(g + 1) * r].reshape(gp, ns).T

    dt2 = dt_ref[...]
    da2 = dt2 * (-jnp.exp(alog_ref[...]))
    row = lax.broadcasted_iota(jnp.int32, (lc, lc), 0)
    col = lax.broadcasted_iota(jnp.int32, (lc, lc), 1)
    diff = jnp.where(fwd, row - col, col - row)
    mask = diff >= 0
    tri = jnp.where(mask, 1.0, 0.0).astype(f32)
    acs2 = jnp.dot(tri, da2, precision=HIGHEST, preferred_element_type=f32)
    acst2 = acs2.T
    dtt2 = dt2.T
    acs = jnp.where(fwd, acs2[:, :nh], acs2[:, nh:])
    acst = jnp.where(fwd, acst2[:nh], acst2[nh:])
    dtt = jnp.where(fwd, dtt2[:nh], dtt2[nh:])
    total = jnp.where(fwd, acst[:, lc - 1:lc], acst[:, 0:1])
    totb = jnp.broadcast_to(total, (nh, lc))
    acst_ref[...] = acst
    dtt_ref[...] = dtt
    wt_ref[...] = dtt * jnp.exp(totb - acst)
    cdt_ref[...] = jnp.exp(totb)
    for g in range(ng):
        acsc_ref[g] = acs[:, g * r:(g + 1) * r]
    pair = SLAB_LANES // hp
    first_lanes = lax.broadcasted_iota(jnp.int32, (1, SLAB_LANES), 1) < hp

    def group(g, carry):
        lo = pl.multiple_of(g * gp, gp)
        bg = bc_ref[:, pl.ds(pl.multiple_of(g * ns, ns), ns)]
        cg = bc_ref[:, pl.ds(pl.multiple_of((ng + g) * ns, ns), ns)]
        scores = lax.dot_general(cg, bg, (((1,), (1,)), ((), ())), preferred_element_type=f32)
        bt = bg.astype(f32).T
        cf = cg.astype(f32)
        cols = acsc_ref[g]
        hrow = pl.ds(pl.multiple_of(g * r, r), r)
        rows_a, rows_dt, rows_w, rows_cd = acst_ref[hrow, :], dtt_ref[hrow, :], wt_ref[hrow, :], cdt_ref[hrow, :]
        zero = jnp.zeros((ns, ns), bf16)
        glanes = pl.ds(lo, gp)
        x_g = xs_ref[:, glanes].astype(bf16)
        st_g = state_ref[:, glanes]
        ys, sts = [], []
        for k in range(r // pair):
            st = st_g[:, k * SLAB_LANES:(k + 1) * SLAB_LANES]
            rhs = jnp.concatenate([x_g[:, k * SLAB_LANES:(k + 1) * SLAB_LANES], st.astype(bf16)], axis=0)
            blocks = []
            for q in (pair * k, pair * k + 1):
                a_l = jnp.broadcast_to(cols[:, q:q + 1], (lc, lc))
                seg = a_l - rows_a[q:q + 1, :]
                mh = scores * jnp.where(mask, jnp.exp(seg), 0.0) * rows_dt[q:q + 1, :]
                ch = cf * jnp.exp(a_l)
                bw = bt * rows_w[q:q + 1, :]
                blocks.append(jnp.concatenate([mh.astype(bf16), ch.astype(bf16)], axis=1))
                blocks.append(jnp.concatenate([bw.astype(bf16), zero], axis=1))
            out = jnp.dot(jnp.concatenate(blocks, axis=0), rhs, preferred_element_type=f32)
            h0_rows, h1_rows = out[:lc + ns], out[lc + ns:]
            y = jnp.where(first_lanes, h0_rows[:lc], h1_rows[:lc])
            cs = jnp.where(first_lanes, h0_rows[lc:], h1_rows[lc:])
            cd = jnp.where(first_lanes, rows_cd[pair * k:pair * k + 1, :], rows_cd[pair * k + 1:pair * k + 2, :])
            sts.append(st * cd + cs)
            ys.append(y.astype(y_ref.dtype))
        state_ref[:, glanes] = jnp.concatenate(sts, axis=1)
        y_ref[:, glanes] = jnp.concatenate(ys, axis=1)
        return carry

    lax.fori_loop(0, ng, group, 0, unroll=4)

    @pl.when(jnp.logical_and(is_last, in_prompt))
    def _():
        for g in range(ng):
            st_ref[g * r:(g + 1) * r] = state_ref[:, g * gp:(g + 1) * gp].T.reshape(r, hp, ns)


def _ssd_gate_kernel(y_ref, xs_ref, z_ref, dx_ref, g_ref, o_ref):
    y = y_ref[0].astype(f32) + y_ref[1].astype(f32) + dx_ref[...] * xs_ref[...]
    y = y * _silu(z_ref[...])
    o_ref[...] = _rms(y, g_ref[...]).astype(o_ref.dtype)


def _ssd_mixer(h, j, h0, w_in, w_conv, b_conv, dt_bias, a_log, d_skip, norm_g, w_out, rows, n_prompt):
    m, d = h.shape
    nl = w_in.shape[0]
    n_seq_s, _, _, nh, hp, ns = h0.shape
    di = nh * hp
    conv_dim = w_conv.shape[-1]
    ng = (conv_dim - di) // (2 * ns)
    lc = SSM_CHUNK
    assert rows.seq_p % lc == 0 and rows.seq_s % lc == 0 and nh % ng == 0
    tm = rows.seq_s
    npt = rows.mp // tm

    tnz = _tile(di, 1024)
    tmz = _tile(rows.seq_s, 1024)
    z = _fused_matmul(h, w_in, j, [0], tm=tmz, tn=tnz, n_tiles=di // tnz, extras=[],
                      outs=[_out_tile(m, di, tmz, tnz, f32)], epilogue=_epi_plain, name="ssd_in_z")[0]
    tnx = _tile(di, 512)
    assert di % tnx == 0 and (2 * ng * ns) % tnx == 0
    cw = w_conv.shape[1]
    bcv = b_conv.reshape(nl, 1, conv_dim)
    conv_epi = functools.partial(_conv_silu_epilogue, chunk=rows.chunk, n_prompt_tiles=npt,
                                 seq_p=rows.seq_p, seq_s=rows.seq_s)

    def conv_part(col0, width_cols, dtype, name):
        nt = width_cols // tnx
        woff = (di + col0) // tnx
        coff = col0 // tnx
        return _fused_matmul(
            h, w_in, j, [woff], tm=tm, tn=tnx, n_tiles=nt,
            extras=[(w_conv, pl.BlockSpec((None, cw, tnx), lambda jj, i: (j, 0, coff + jj))),
                    (bcv, _bias_spec(j, tnx, coff))],
            outs=[_out_tile(m, width_cols, tm, tnx, dtype)], epilogue=conv_epi, chunk=rows.chunk, name=name)[0]

    xs = conv_part(0, di, f32, "ssd_in_x")
    bc = conv_part(di, 2 * ng * ns, bf16, "ssd_in_bc")
    ndt = 2 * nh
    dt = _fused_matmul(h, w_in, j, [(di + conv_dim) // ndt], tm=tmz, tn=ndt, n_tiles=1,
                       extras=[(dt_bias.reshape(nl, 1, ndt), _bias_spec(j, ndt))],
                       outs=[_out_tile(m, ndt, tmz, ndt, f32)], epilogue=_softplus_epilogue, name="ssd_in_dt")[0]

    tp, ts = rows.mp // lc, rows.ms // lc
    ncp, ncs = rows.seq_p // lc, rows.seq_s // lc
    nt = tp + ts
    assert lc == SLAB_LANES and ns == lc and 2 * hp == SLAB_LANES and (nh // ng) % 2 == 0 and ng % 2 == 0
    chunk = lambda dd, t: jnp.where(dd == 0, t, nt - 1 - t)
    rowmap = lambda dd, t: (chunk(dd, t), 0)
    y2, new_state = pl.pallas_call(
        functools.partial(_ssd_scan_kernel, nh=nh, hp=hp, ns=ns, ng=ng, tp=tp, ncp=ncp, ncs=ncs),
        grid=(2, nt),
        in_specs=[pl.BlockSpec((lc, di), rowmap),
                  pl.BlockSpec((lc, 2 * ng * ns), rowmap),
                  pl.BlockSpec((lc, ndt), rowmap),
                  pl.BlockSpec((None, 1, ndt), lambda dd, t: (j, 0, 0)),
                  pl.BlockSpec((None, None, None, nh, hp, ns),
                               lambda dd, t: (jnp.clip((chunk(dd, t) - tp) // ncs, 0, n_seq_s - 1), j, dd, 0, 0, 0))],
        out_specs=[pl.BlockSpec((None, lc, di), lambda dd, t: (dd, chunk(dd, t), 0)),
                   pl.BlockSpec((None, None, nh, hp, ns),
                                lambda dd, t: (jnp.clip(chunk(dd, t) // ncp, 0, n_prompt - 1), dd, 0, 0, 0))],
        out_shape=[jax.ShapeDtypeStruct((2, m, di), bf16),
                   jax.ShapeDtypeStruct((n_prompt, 2, nh, hp, ns), f32)],
        scratch_shapes=[pltpu.VMEM((ns, di), f32)] + [pltpu.VMEM((nh, lc), f32)] * 4
                       + [pltpu.VMEM((ng, lc, nh // ng), f32)],
        compiler_params=_params(2), name="ssd_scan",
    )(xs, bc, dt, a_log.reshape(nl, 1, ndt), h0)

    tg = _tile(rows.seq_s, 256)
    vrow = pl.BlockSpec((tg, di), lambda i: (i, 0))
    gated = pl.pallas_call(
        _ssd_gate_kernel, grid=(m // tg,),
        in_specs=[pl.BlockSpec((2, tg, di), lambda i: (0, i, 0)), vrow, vrow,
                  pl.BlockSpec((None, 1, di), lambda i: (j, 0, 0)),
                  pl.BlockSpec((None, 1, di), lambda i: (j, 0, 0))],
        out_specs=vrow, out_shape=jax.ShapeDtypeStruct((m, di), bf16),
        compiler_params=_params(1), name="ssd_gate",
    )(y2, xs, z, jnp.repeat(d_skip, hp, axis=1).reshape(nl, 1, di), norm_g.reshape(nl, 1, di))
    tmo, tno = _tile(m, 512), _tile(d, 512)
    out = _fused_matmul(gated, w_out, j, [0], tm=tmo, tn=tno, n_tiles=d // tno, extras=[],
                        outs=[_out_tile(m, d, tmo, tno, f32)], epilogue=_epi_plain, name="ssd_out")[0]
    return out, new_state


def _rope(x, cos, sin):
    n = x.shape[-1]
    lane = lax.broadcasted_iota(jnp.int32, x.shape, x.ndim - 1)
    partner = jnp.where((lane & 1) == 0, pltpu.roll(x, n - 1, axis=x.ndim - 1), pltpu.roll(x, 1, axis=x.ndim - 1))
    return x * cos + partner * sin


def _q_epilogue(accs, e_refs, o_refs, i, *, hd, scale):
    gain_ref, cos_ref, sin_ref = e_refs
    acc = accs[0]
    cos, sin, gain = cos_ref[...], sin_ref[...], gain_ref[...]
    for hh in range(acc.shape[1] // hd):
        q = _rope(_rms(acc[:, hh * hd:(hh + 1) * hd], gain), cos, sin) * scale
        o_refs[0][:, hh * hd:(hh + 1) * hd] = q.astype(o_refs[0].dtype)


def _kv_epilogue(accs, e_refs, o_refs, i, *, hd, nkv):
    gain_ref, cos_ref, sin_ref = e_refs
    acc = accs[0]
    cos, sin, gain = cos_ref[...], sin_ref[...], gain_ref[...]
    o_refs[1][...] = acc
    for hh in range(nkv):
        kn = _rms(acc[:, hh * hd:(hh + 1) * hd], gain)
        o_refs[1][:, hh * hd:(hh + 1) * hd] = kn
        o_refs[0][:, hh * hd:(hh + 1) * hd] = _rope(kn, cos, sin).astype(o_refs[0].dtype)
    o_refs[0][:, nkv * hd:] = acc[:, nkv * hd:].astype(o_refs[0].dtype)


def _attn_kernel(q_ref, k_ref, v_ref, *rest, group, hd, cached):
    if cached:
        ck_ref, cv_ref, o_ref = rest
        ck = ck_ref[...].astype(bf16)
        cv = cv_ref[...].astype(bf16)
    else:
        (o_ref,) = rest
    k = k_ref[...]
    v = v_ref[...]
    nt = (((1,), (1,)), ((), ()))
    for g in range(group):
        q = q_ref[:, g * hd:(g + 1) * hd]
        s = lax.dot_general(q, k, nt, preferred_element_type=f32)
        mx = jnp.max(s, axis=-1, keepdims=True)
        if cached:
            s2 = lax.dot_general(q, ck, nt, preferred_element_type=f32)
            mx = jnp.maximum(mx, jnp.max(s2, axis=-1, keepdims=True))
        p = jnp.exp(s - mx)
        den = jnp.sum(p, axis=-1, keepdims=True)
        o = jnp.dot(p.astype(bf16), v, preferred_element_type=f32)
        if cached:
            p2 = jnp.exp(s2 - mx)
            den = den + jnp.sum(p2, axis=-1, keepdims=True)
            o = o + jnp.dot(p2.astype(bf16), cv, preferred_element_type=f32)
        o_ref[:, g * hd:(g + 1) * hd] = (o / den).astype(o_ref.dtype)


def _attention(h, j, cache_k, cache_v, w_qkv, q_norm, k_norm, w_o, rows, n_prompt):
    m, d = h.shape
    nl = w_qkv.shape[0]
    hd = q_norm.shape[-1]
    n_seq_s, _, past, nkv, _ = cache_k.shape
    nq = w_o.shape[1] // hd
    group = nq // nkv
    assert rows.seq_s % GRID_W == 0
    pos = jnp.arange(rows.seq_s)
    inv = ROPE_THETA ** (-jnp.arange(hd // 4, dtype=f32) / (hd // 4))
    ang = jnp.concatenate([(pos // GRID_W).astype(f32)[:, None] * inv, (pos % GRID_W).astype(f32)[:, None] * inv], -1)
    cos_s = jnp.repeat(jnp.cos(ang), 2, axis=-1)
    sin_s = jnp.stack([-jnp.sin(ang), jnp.sin(ang)], axis=-1).reshape(rows.seq_s, hd)
    reps = rows.ms // rows.seq_s
    cos_t = jnp.concatenate([jnp.ones((rows.mp, hd), f32), jnp.tile(cos_s, (reps, 1))], 0)
    sin_t = jnp.concatenate([jnp.zeros((rows.mp, hd), f32), jnp.tile(sin_s, (reps, 1))], 0)

    tm = _tile(rows.seq_s, 1024)
    tab = pl.BlockSpec((tm, hd), lambda jj, i: (i, 0))
    gspec = pl.BlockSpec((None, 1, hd), lambda jj, i: (j, 0, 0))
    nqc, nkc = nq * hd, nkv * hd
    tnq = _tile(nqc, 1024)
    q = _fused_matmul(h, w_qkv, j, [0], tm=tm, tn=tnq, n_tiles=nqc // tnq,
                      extras=[(q_norm.reshape(nl, 1, hd), gspec), (cos_t, tab), (sin_t, tab)],
                      outs=[_out_tile(m, nqc, tm, tnq, bf16)],
                      epilogue=functools.partial(_q_epilogue, hd=hd, scale=hd ** -0.5), name="attn_q")[0]
    tnk = 2 * nkc
    assert nqc % tnk == 0
    kv, kv_f32 = _fused_matmul(h, w_qkv, j, [nqc // tnk], tm=tm, tn=tnk, n_tiles=1,
                               extras=[(k_norm.reshape(nl, 1, hd), gspec), (cos_t, tab), (sin_t, tab)],
                               outs=[_out_tile(m, tnk, tm, tnk, bf16), _out_tile(m, tnk, tm, tnk, f32)],
                               epilogue=functools.partial(_kv_epilogue, hd=hd, nkv=nkv), name="attn_kv")

    gw = group * hd
    sp = rows.seq_p
    o_p = pl.pallas_call(
        functools.partial(_attn_kernel, group=group, hd=hd, cached=False),
        grid=(n_prompt, nkv),
        in_specs=[pl.BlockSpec((sp, gw), lambda b, kh: (b, kh)),
                  pl.BlockSpec((sp, hd), lambda b, kh: (b, kh)),
                  pl.BlockSpec((sp, hd), lambda b, kh: (b, nkv + kh))],
        out_specs=pl.BlockSpec((sp, gw), lambda b, kh: (b, kh)),
        out_shape=jax.ShapeDtypeStruct((rows.mp, nqc), bf16),
        compiler_params=_params(2), name="attn_prompt",
    )(q, kv, kv)
    ss = rows.seq_s
    tq = _tile(ss, 512)
    nqb = ss // tq
    pb = rows.mp // ss
    pbq = rows.mp // tq
    ck = cache_k[:, j].reshape(n_seq_s, past, nkc)
    cv = cache_v[:, j].reshape(n_seq_s, past, nkc)
    o_s = pl.pallas_call(
        functools.partial(_attn_kernel, group=group, hd=hd, cached=True),
        grid=(n_seq_s, nkv, nqb),
        in_specs=[pl.BlockSpec((tq, gw), lambda b, kh, qi: (pbq + b * nqb + qi, kh)),
                  pl.BlockSpec((ss, hd), lambda b, kh, qi: (pb + b, kh)),
                  pl.BlockSpec((ss, hd), lambda b, kh, qi: (pb + b, nkv + kh)),
                  pl.BlockSpec((None, past, hd), lambda b, kh, qi: (b, 0, kh)),
                  pl.BlockSpec((None, past, hd), lambda b, kh, qi: (b, 0, kh))],
        out_specs=pl.BlockSpec((tq, gw), lambda b, kh, qi: (b * nqb + qi, kh)),
        out_shape=jax.ShapeDtypeStruct((rows.ms, nqc), bf16),
        compiler_params=_params(3), name="attn_sample",
    )(q, kv, kv, ck, cv)
    o = jnp.concatenate([o_p, o_s], axis=0)
    tmo, tno = _tile(m, 1024), _tile(d, 1024)
    out = _fused_matmul(o, w_o, j, [0], tm=tmo, tn=tno, n_tiles=d // tno, extras=[],
                        outs=[_out_tile(m, d, tmo, tno, f32)], epilogue=_epi_plain, name="attn_o")[0]
    new_k = kv_f32[:rows.mp, :nkc].reshape(n_prompt, sp, nkv, hd)
    new_v = kv_f32[:rows.mp, nkc:].reshape(n_prompt, sp, nkv, hd)
    return out, new_k, new_v


def kernel(x_prompt, x_sample, c, state_ssd, cache_k, cache_v, c_ctx, w_mod, b_mod, norm_pre, norm_post, cv_w_pw1, cv_b_pw1, cv_w_dw, cv_b_dw, cv_ln_g, cv_ln_b, cv_w_pw2, cv_b_pw2, ssd_w_in, ssd_w_conv, ssd_b_conv, ssd_dt_bias, ssd_a_log, ssd_d, ssd_norm_g, ssd_w_out, attn_w_qkv, attn_q_norm, attn_k_norm, attn_w_o, ffn_w_up, ffn_w_dw, ffn_b_dw, ffn_w_down):
    nb, sp, d = x_prompt.shape
    ndb, ss, _ = x_sample.shape
    depth = w_mod.shape[0]
    rows = _Rows(nb * sp, ndb * ss, sp, ss)
    assert sp & (sp - 1) == 0 and ss & (ss - 1) == 0, "sequence lengths must be powers of two"
    assert rows.mp % ss == 0 and 1 + ndb <= COND_ROWS

    y = jnp.concatenate([x_prompt.reshape(rows.mp, d), x_sample.reshape(rows.ms, d)], axis=0)
    cond = jnp.concatenate([c_ctx[None, :], c, jnp.zeros((COND_ROWS - 1 - ndb, d), f32)], axis=0)
    mod = _modulation(cond, w_mod, b_mod).reshape(depth * COND_ROWS, 6, d)
    gpre = norm_pre.reshape(depth * 2, 1, d)
    gpost = norm_post.reshape(depth * 2, 1, d)
    h = _prenorm(y, gpre, 0, mod, 0, 1, 0, rows)
    new_ssd, new_k, new_v = [], [], []
    for i in range(depth):
        kind, j = i % N_MIXERS, i // N_MIXERS
        if kind == 0:
            o = _conformer(h, j, cv_w_pw1, cv_b_pw1, cv_w_dw, cv_b_dw, cv_ln_g, cv_ln_b, cv_w_pw2, cv_b_pw2, rows)
        elif kind == 1:
            o, st = _ssd_mixer(h, j, state_ssd, ssd_w_in, ssd_w_conv, ssd_b_conv, ssd_dt_bias, ssd_a_log,
                               ssd_d, ssd_norm_g, ssd_w_out, rows, nb)
            new_ssd.append(st)
        else:
            o, k_new, v_new = _attention(h, j, cache_k, cache_v, attn_w_qkv, attn_q_norm, attn_k_norm,
                                         attn_w_o, rows, nb)
            new_k.append(k_new)
            new_v.append(v_new)
        y, h = _resid(y, o, gpost, 2 * i, mod, i, 2, rows, nxt=(gpre, 2 * i + 1, i, 4, 3))
        o = _conv_ffn(h, i, ffn_w_up, ffn_w_dw, ffn_b_dw, ffn_w_down, rows)
        if i + 1 < depth:
            y, h = _resid(y, o, gpost, 2 * i + 1, mod, i, 5, rows, nxt=(gpre, 2 * i + 2, i + 1, 1, 0))
        else:
            y = _resid(y, o, gpost, 2 * i + 1, mod, i, 5, rows)
    yp = y[:rows.mp].reshape(nb, sp, d)
    ys = y[rows.mp:].reshape(ndb, ss, d)
    return (yp, ys, jnp.stack(new_ssd, axis=1), jnp.stack(new_k, axis=1), jnp.stack(new_v, axis=1))
```

```python
import functools
import math

import jax
import jax.numpy as jnp
from jax import lax
from jax.experimental import pallas as pl
from jax.experimental.pallas import tpu as pltpu

f32 = jnp.float32
bf16 = jnp.bfloat16
HIGHEST = lax.Precision.HIGHEST

EPS = 1e-6
GRID_W = 64
ROPE_THETA = 10000.0
SSM_CHUNK = 128
N_MIXERS = 3
V7X_VMEM_LIMIT_BYTES = 56 * 1024 * 1024
COND_ROWS = 8
ROW_HALO = 8
ROW_CHUNK = 256
SLAB_LANES = 128
EPILOGUE_LAG = 2


def _params(n_axes, vmem=V7X_VMEM_LIMIT_BYTES):
    return pltpu.CompilerParams(dimension_semantics=("arbitrary",) * n_axes,
                                vmem_limit_bytes=vmem)


def _tile(dim, pref):
    return pref if dim % pref == 0 else dim


def _sigmoid(x):
    return 1.0 / (1.0 + jnp.exp(-x))


def _silu(x):
    return x * _sigmoid(x)


def _rms(x, g):
    return x * lax.rsqrt(jnp.mean(x * x, axis=-1, keepdims=True) + EPS) * g


def _mod_kernel(c_ref, w_ref, b_ref, o_ref):
    s = _silu(c_ref[...]).astype(bf16)
    o_ref[...] = jnp.dot(s, w_ref[...].astype(bf16), preferred_element_type=f32) + b_ref[...]


def _modulation(cond, w_mod, b_mod):
    depth, d, n = w_mod.shape
    tn = _tile(n, 1024)
    return pl.pallas_call(
        _mod_kernel,
        grid=(depth, n // tn),
        in_specs=[pl.BlockSpec((COND_ROWS, d), lambda l, j: (0, 0)),
                  pl.BlockSpec((None, d, tn), lambda l, j: (l, 0, j)),
                  pl.BlockSpec((None, 1, tn), lambda l, j: (l, 0, j))],
        out_specs=pl.BlockSpec((None, COND_ROWS, tn), lambda l, j: (l, 0, j)),
        out_shape=jax.ShapeDtypeStruct((depth, COND_ROWS, n), f32),
        compiler_params=_params(2), name="modulation",
    )(cond, w_mod, b_mod.reshape(depth, 1, n))


def _prenorm_kernel(yp_ref, ys_ref, g_ref, mod_ref, h_ref, *, sc, sh, n_prompt_tiles):
    y = jnp.where(pl.program_id(0) < n_prompt_tiles, yp_ref[...], ys_ref[...])
    yn = _rms(y, g_ref[...])
    h_ref[...] = (yn * (1.0 + mod_ref[sc:sc + 1, :]) + mod_ref[sh:sh + 1, :]).astype(h_ref.dtype)


def _resid_kernel(*refs, ga, nxt, split_in, split_out, n_prompt_tiles):
    i = pl.program_id(0)
    if split_in:
        yp_ref, ys_ref, o_ref, gpost_ref, moda_ref, *rest = refs
        y_in = jnp.where(i < n_prompt_tiles, yp_ref[...], ys_ref[...])
    else:
        y_ref, o_ref, gpost_ref, moda_ref, *rest = refs
        y_in = y_ref[...]
    y = y_in + moda_ref[ga:ga + 1, :] * _rms(o_ref[...], gpost_ref[...])
    if split_out:
        outp_ref, outs_ref = rest

        @pl.when(i < n_prompt_tiles)
        def _():
            outp_ref[...] = y

        @pl.when(i >= n_prompt_tiles)
        def _():
            outs_ref[...] = y
    elif nxt is None:
        (ynew_ref,) = rest
        ynew_ref[...] = y
    else:
        gpre_ref, modb_ref, ynew_ref, h_ref = rest
        sc, sh = nxt
        ynew_ref[...] = y
        yn = _rms(y, gpre_ref[...])
        h_ref[...] = (yn * (1.0 + modb_ref[sc:sc + 1, :]) + modb_ref[sh:sh + 1, :]).astype(h_ref.dtype)


class _Rows:
    def __init__(self, mp, ms, seq_p, seq_s):
        self.mp, self.ms, self.m = mp, ms, mp + ms
        self.seq_p, self.seq_s = seq_p, seq_s
        self.chunk = _tile(seq_p, ROW_CHUNK)
        assert seq_p % self.chunk == 0 and seq_s % self.chunk == 0

    def cond_of_tile(self, tm):
        assert self.mp % tm == 0 and self.seq_s % tm == 0
        npt, per = self.mp // tm, self.seq_s // tm
        return lambda i: jnp.where(i < npt, 0, 1 + (i - npt) // per)


def _split_specs(rows, tm, d):
    npt = rows.mp // tm
    return [pl.BlockSpec((tm, d), lambda i: (jnp.minimum(i, npt - 1), 0)),
            pl.BlockSpec((tm, d), lambda i: (jnp.maximum(i - npt, 0), 0))]


def _prenorm(y_pair, g_all, g_idx, mod, layer, sc, sh, rows):
    d = y_pair[0].shape[1]
    m = rows.m
    tm = _tile(rows.seq_s, 256)
    cond = rows.cond_of_tile(tm)
    return pl.pallas_call(
        functools.partial(_prenorm_kernel, sc=sc, sh=sh, n_prompt_tiles=rows.mp // tm),
        grid=(m // tm,),
        in_specs=_split_specs(rows, tm, d)
                 + [pl.BlockSpec((None, 1, d), lambda i: (g_idx, 0, 0)),
                    pl.BlockSpec((None, 6, d), lambda i: (layer * COND_ROWS + cond(i), 0, 0))],
        out_specs=pl.BlockSpec((tm, d), lambda i: (i, 0)),
        out_shape=jax.ShapeDtypeStruct((m, d), bf16),
        compiler_params=_params(1), name="prenorm",
    )(*y_pair, g_all, mod)


def _resid(y, o, gpost_all, gpost_idx, mod, layer, ga, rows, nxt=None, split_out=False):
    m, d = o.shape
    tm = _tile(rows.seq_s, 256)
    cond = rows.cond_of_tile(tm)
    row_spec = pl.BlockSpec((tm, d), lambda i: (i, 0))
    split_in = isinstance(y, tuple)
    in_specs = (_split_specs(rows, tm, d) if split_in else [row_spec]) + [
        row_spec,
        pl.BlockSpec((None, 1, d), lambda i: (gpost_idx, 0, 0)),
        pl.BlockSpec((None, 6, d), lambda i: (layer * COND_ROWS + cond(i), 0, 0))]
    args = (list(y) if split_in else [y]) + [o, gpost_all, mod]
    if split_out:
        assert nxt is None
        out_specs = _split_specs(rows, tm, d)
        out_shape = [jax.ShapeDtypeStruct((rows.mp, d), f32), jax.ShapeDtypeStruct((rows.ms, d), f32)]
        kn = None
    elif nxt is None:
        out_specs, out_shape, kn = row_spec, jax.ShapeDtypeStruct((m, d), f32), None
    else:
        gpre_all, gpre_idx, nlayer, sc, sh = nxt
        in_specs += [pl.BlockSpec((None, 1, d), lambda i: (gpre_idx, 0, 0)),
                     pl.BlockSpec((None, 6, d), lambda i: (nlayer * COND_ROWS + cond(i), 0, 0))]
        args += [gpre_all, mod]
        out_specs = [row_spec, row_spec]
        out_shape = [jax.ShapeDtypeStruct((m, d), f32), jax.ShapeDtypeStruct((m, d), bf16)]
        kn = (sc, sh)
    return pl.pallas_call(
        functools.partial(_resid_kernel, ga=ga, nxt=kn, split_in=split_in, split_out=split_out,
                          n_prompt_tiles=rows.mp // tm),
        grid=(m // tm,), in_specs=in_specs, out_specs=out_specs, out_shape=out_shape,
        compiler_params=_params(1), name="resid",
    )(*args)


def _fmm_kernel(*refs, n_slab, n_extra, n_out, epilogue, chunk):
    x_ref = refs[0]
    w_refs = refs[1:1 + n_slab]
    e_refs = refs[1 + n_slab:1 + n_slab + n_extra]
    o_refs = refs[1 + n_slab + n_extra:1 + n_slab + n_extra + n_out]
    wb_refs = refs[1 + n_slab + n_extra + n_out:1 + 2 * n_slab + n_extra + n_out]
    u_refs = refs[1 + 2 * n_slab + n_extra + n_out:]
    i = pl.program_id(1)

    @pl.when(i == 0)
    def _():
        for w_ref, wb_ref in zip(w_refs, wb_refs):
            wb_ref[...] = w_ref[...].astype(bf16)

    if chunk is None:
        x = x_ref[...]
        accs = [jnp.dot(x, wb_ref[...], preferred_element_type=f32) for wb_ref in wb_refs]
        epilogue(accs, e_refs, o_refs, i)
        return

    tm = x_ref.shape[0]
    n_chunk = tm // chunk
    per_slab = [u_refs[s * n_chunk:(s + 1) * n_chunk] for s in range(n_slab)]
    zeros = jnp.zeros((ROW_HALO, u_refs[0].shape[1]), f32)
    for bufs in per_slab:
        bufs[0][0:ROW_HALO, :] = zeros
        bufs[-1][ROW_HALO + chunk:, :] = zeros
    for c in range(n_chunk + EPILOGUE_LAG):
        if c < n_chunk:
            xc = x_ref[c * chunk:(c + 1) * chunk, :]
            for wb_ref, bufs in zip(wb_refs, per_slab):
                prod = jnp.dot(xc, wb_ref[...], preferred_element_type=f32)
                bufs[c][ROW_HALO:ROW_HALO + chunk, :] = prod
                if c > 0:
                    bufs[c - 1][ROW_HALO + chunk:, :] = prod[0:ROW_HALO]
                if c + 1 < n_chunk:
                    bufs[c + 1][0:ROW_HALO, :] = prod[chunk - ROW_HALO:]
        if c >= EPILOGUE_LAG:
            done = c - EPILOGUE_LAG
            epilogue(done, [bufs[done] for bufs in per_slab], e_refs, o_refs, i)


def _fused_matmul(x, w_all, layer, slab_offs, *, tm, tn, n_tiles, extras, outs, epilogue, name, chunk=None):
    m, k = x.shape
    in_specs = [pl.BlockSpec((tm, k), lambda j, i: (i, 0))]
    args = [x]
    for off in slab_offs:
        in_specs.append(pl.BlockSpec((None, k, tn), lambda j, i, off=off: (layer, 0, off + j)))
        args.append(w_all)
    for arr, spec in extras:
        in_specs.append(spec)
        args.append(arr)
    scratch = [pltpu.VMEM((k, tn), bf16) for _ in slab_offs]
    if chunk is not None:
        assert tm % chunk == 0
        scratch += [pltpu.VMEM((chunk + 2 * ROW_HALO, tn), f32)] * (len(slab_offs) * (tm // chunk))
    return pl.pallas_call(
        functools.partial(_fmm_kernel, n_slab=len(slab_offs), n_extra=len(extras),
                          n_out=len(outs), epilogue=epilogue, chunk=chunk),
        grid=(n_tiles, m // tm),
        in_specs=in_specs,
        out_specs=[spec for _, spec in outs],
        out_shape=[sds for sds, _ in outs],
        scratch_shapes=scratch,
        compiler_params=_params(2), name=name,
    )(*args)


def _bias_spec(layer, tn, off=0):
    return pl.BlockSpec((None, 1, tn), lambda j, i: (layer, 0, off + j))


def _out_tile(m, n, tm, tn, dtype):
    return (jax.ShapeDtypeStruct((m, n), dtype), pl.BlockSpec((tm, tn), lambda j, i: (i, j)))


def _epi_plain(accs, e_refs, o_refs, i):
    o_refs[0][...] = accs[0].astype(o_refs[0].dtype)


def _epi_bias(accs, e_refs, o_refs, i):
    o_refs[0][...] = (accs[0] + e_refs[0][...]).astype(o_refs[0].dtype)


def _dwconv_chunk(u_ref, w, bias, c, chunk, seq):
    width = w.shape[0]
    left = (width - 1) // 2
    assert left <= ROW_HALO and width - 1 - left <= ROW_HALO and chunk >= 2 * ROW_HALO
    tn = u_ref.shape[1]
    base = ROW_HALO
    starts_seq = ((c * chunk) & (seq - 1)) == 0
    ends_seq = (((c + 1) * chunk) & (seq - 1)) == 0
    far = chunk + ROW_HALO
    w_tile = [jnp.broadcast_to(w[k:k + 1, :], (ROW_HALO, tn)) for k in range(width)]
    w_full = [jnp.tile(wt, (chunk // ROW_HALO, 1)) for wt in w_tile]
    b_tile = jnp.broadcast_to(bias, (ROW_HALO, tn))
    b_full = jnp.tile(b_tile, (chunk // ROW_HALO, 1))

    def taps(r0, n_rows, masked):
        w_rows = w_tile if n_rows == ROW_HALO else w_full
        acc = b_tile if n_rows == ROW_HALO else b_full
        if masked:
            row = lax.broadcasted_iota(jnp.int32, (n_rows, tn), 0) + r0
        for k in range(width):
            d = k - left
            v = u_ref[pl.ds(base + r0 + d, n_rows), :]
            if masked and d < 0:
                v = jnp.where(row >= jnp.where(starts_seq, -d, -far), v, 0.0)
            if masked and d > 0:
                v = jnp.where(row < jnp.where(ends_seq, chunk - d, far), v, 0.0)
            term = v * w_rows[k]
            acc = term if acc is None else acc + term
        return acc

    body = taps(0, chunk, False)
    top = taps(0, ROW_HALO, True)
    bottom = taps(chunk - ROW_HALO, ROW_HALO, True)
    return jnp.concatenate([top, body[ROW_HALO:chunk - ROW_HALO], bottom], axis=0)


def _ffn_up_epilogue(c, u_refs, e_refs, o_refs, i, *, chunk, n_prompt_tiles, seq_p, seq_s):
    wa_ref, wg_ref, ba_ref, bg_ref = e_refs
    seq = jnp.where(i < n_prompt_tiles, seq_p, seq_s)
    a = _dwconv_chunk(u_refs[0], wa_ref[...], ba_ref[...], c, chunk, seq)
    g = _dwconv_chunk(u_refs[1], wg_ref[...], bg_ref[...], c, chunk, seq)
    o_refs[0][c * chunk:(c + 1) * chunk, :] = (_silu(g) * a).astype(o_refs[0].dtype)


def _conv_ffn(h, layer, w_up, w_dw, b_dw, w_down, rows):
    m, d = h.shape
    dff = w_down.shape[1]
    depth = w_up.shape[0]
    tm = rows.seq_s
    tn = _tile(dff, 512)
    nt = dff // tn
    width = w_dw.shape[1]
    b3 = b_dw.reshape(depth, 1, 2 * dff)
    dw_spec = lambda off: pl.BlockSpec((None, width, tn), lambda j, i: (layer, 0, off + j))
    act = _fused_matmul(
        h, w_up, layer, [0, nt], tm=tm, tn=tn, n_tiles=nt,
        extras=[(w_dw, dw_spec(0)), (w_dw, dw_spec(nt)), (b3, _bias_spec(layer, tn)), (b3, _bias_spec(layer, tn, nt))],
        outs=[_out_tile(m, dff, tm, tn, bf16)], chunk=rows.chunk,
        epilogue=functools.partial(_ffn_up_epilogue, chunk=rows.chunk, n_prompt_tiles=rows.mp // tm,
                                   seq_p=rows.seq_p, seq_s=rows.seq_s),
        name="ffn_up")[0]
    tm2, tn2 = _tile(m, 512), _tile(d, 512)
    return _fused_matmul(act, w_down, layer, [0], tm=tm2, tn=tn2, n_tiles=d // tn2, extras=[],
                         outs=[_out_tile(m, d, tm2, tn2, f32)], epilogue=_epi_plain, name="ffn_down")[0]


def _glu_epilogue(accs, e_refs, o_refs, i):
    a = accs[0] + e_refs[0][...]
    g = accs[1] + e_refs[1][...]
    o_refs[0][...] = (a * _sigmoid(g)).astype(o_refs[0].dtype)


CONV_HALO = 16
CONV_LANES = 128
CONV_ROWS = 128
LN_ROWS = 32


def _cv_mid_kernel(u_ref, prev_ref, next_ref, w_ref, b_ref, g_ref, beta_ref, o_ref, pad_ref, conv_ref,
                   *, blocks_p, per_seq_p, per_seq_s):
    i = pl.program_id(0)
    rb, d = u_ref.shape
    width = w_ref.shape[0]
    left = (width - 1) // 2
    per = jnp.where(i < blocks_p, per_seq_p, per_seq_s)
    k_in_seq = jnp.where(i < blocks_p, i, i - blocks_p) % per
    first = k_in_seq == 0
    last = k_in_seq == per - 1
    pad_ref[0:CONV_HALO, :] = jnp.where(first, 0.0, prev_ref[...])
    pad_ref[CONV_HALO:CONV_HALO + rb, :] = u_ref[...]
    pad_ref[CONV_HALO + rb:CONV_HALO + rb + CONV_HALO, :] = jnp.where(last, 0.0, next_ref[...])

    def lane_chunk(cidx, carry):
        lo = pl.multiple_of(cidx * CONV_LANES, CONV_LANES)
        wc = w_ref[:, pl.ds(lo, CONV_LANES)]
        for r0 in range(0, rb, CONV_ROWS):
            partial = []
            for phase in range(ROW_HALO):
                ks = [k for k in range(width) if (CONV_HALO - left + k) % ROW_HALO == phase]
                if not ks:
                    continue
                span = max(CONV_HALO - left + k for k in ks) - phase
                shifted = pad_ref[pl.ds(r0 + phase, CONV_ROWS + span), pl.ds(lo, CONV_LANES)]
                acc = None
                for k in ks:
                    off = CONV_HALO - left + k - phase
                    term = shifted[off:off + CONV_ROWS] * wc[k:k + 1, :]
                    acc = term if acc is None else acc + term
                partial.append(acc)
            total = partial[0]
            for acc in partial[1:]:
                total = total + acc
            conv_ref[pl.ds(r0, CONV_ROWS), pl.ds(lo, CONV_LANES)] = total
        return carry

    lax.fori_loop(0, d // CONV_LANES, lane_chunk, 0)

    def norm_rows(ridx, carry):
        rs = pl.ds(pl.multiple_of(ridx * LN_ROWS, LN_ROWS), LN_ROWS)
        u = conv_ref[rs, :] + b_ref[...]
        mu = jnp.mean(u, axis=-1, keepdims=True)
        uc = u - mu
        v = uc * lax.rsqrt(jnp.mean(uc * uc, axis=-1, keepdims=True) + EPS) * g_ref[...] + beta_ref[...]
        o_ref[rs, :] = _silu(v).astype(o_ref.dtype)
        return carry

    lax.fori_loop(0, rb // LN_ROWS, norm_rows, 0)


def _conformer(h, j, w_pw1, b_pw1, w_dw, b_dw, ln_g, ln_b, w_pw2, b_pw2, rows):
    m, d = h.shape
    nl = w_pw1.shape[0]
    tm, tn = _tile(rows.seq_s, 1024), _tile(d, 512)
    nt = d // tn
    b1 = b_pw1.reshape(nl, 1, 2 * d)
    glu = _fused_matmul(h, w_pw1, j, [0, nt], tm=tm, tn=tn, n_tiles=nt,
                        extras=[(b1, _bias_spec(j, tn)), (b1, _bias_spec(j, tn, nt))],
                        outs=[_out_tile(m, d, tm, tn, f32)], epilogue=_glu_epilogue, name="cv_pw1")[0]
    width = w_dw.shape[1]
    assert (width - 1) // 2 <= CONV_HALO
    rb = _tile(rows.seq_p, 256)
    assert rows.seq_p % rb == 0 and rows.seq_s % rb == 0 and rb % CONV_ROWS == 0 and d % CONV_LANES == 0
    hb = rb // CONV_HALO
    n_halo = m // CONV_HALO
    vec = lambda a: a.reshape(nl, 1, d)
    vspec = pl.BlockSpec((None, 1, d), lambda i: (j, 0, 0))
    mid = pl.pallas_call(
        functools.partial(_cv_mid_kernel, blocks_p=rows.mp // rb, per_seq_p=rows.seq_p // rb,
                          per_seq_s=rows.seq_s // rb),
        grid=(m // rb,),
        in_specs=[pl.BlockSpec((rb, d), lambda i: (i, 0)),
                  pl.BlockSpec((CONV_HALO, d), lambda i: (jnp.maximum(i * hb - 1, 0), 0)),
                  pl.BlockSpec((CONV_HALO, d), lambda i: (jnp.minimum((i + 1) * hb, n_halo - 1), 0)),
                  pl.BlockSpec((None, width, d), lambda i: (j, 0, 0)),
                  vspec, vspec, vspec],
        out_specs=pl.BlockSpec((rb, d), lambda i: (i, 0)),
        out_shape=jax.ShapeDtypeStruct((m, d), bf16),
        scratch_shapes=[pltpu.VMEM((rb + 2 * CONV_HALO, d), f32), pltpu.VMEM((rb, d), f32)],
        compiler_params=_params(1), name="cv_mid",
    )(glu, glu, glu, w_dw, vec(b_dw), vec(ln_g), vec(ln_b))
    tm2, tn2 = _tile(m, 1024), _tile(d, 1024)
    return _fused_matmul(mid, w_pw2, j, [0], tm=tm2, tn=tn2, n_tiles=d // tn2,
                         extras=[(vec(b_pw2), _bias_spec(j, tn2))],
                         outs=[_out_tile(m, d, tm2, tn2, f32)], epilogue=_epi_bias, name="cv_pw2")[0]


def _conv_silu_epilogue(c, u_refs, e_refs, o_refs, i, *, chunk, n_prompt_tiles, seq_p, seq_s):
    w_ref, b_ref = e_refs
    seq = jnp.where(i < n_prompt_tiles, seq_p, seq_s)
    u = _dwconv_chunk(u_refs[0], w_ref[...], b_ref[...], c, chunk, seq)
    o_refs[0][c * chunk:(c + 1) * chunk, :] = _silu(u).astype(o_refs[0].dtype)


def _softplus_epilogue(accs, e_refs, o_refs, i):
    x = accs[0] + e_refs[0][...]
    o_refs[0][...] = jnp.maximum(x, 0.0) + jnp.log1p(jnp.exp(-jnp.abs(x)))


def _ssd_scan_kernel(xs_ref, bc_ref, dt_ref, alog_ref, h0_ref, y_ref, st_ref,
                     state_ref, acst_ref, dtt_ref, wt_ref, cdt_ref, acsc_ref,
                     *, nh, hp, ns, ng, tp, ncp, ncs):
    d = pl.program_id(0)
    t = pl.program_id(1)
    nt = pl.num_programs(1)
    lc = xs_ref.shape[0]
    r = nh // ng
    gp = r * hp
    fwd = d == 0
    te = jnp.where(fwd, t, nt - 1 - t)
    in_prompt = te < tp
    c_in_seq = jnp.where(in_prompt, te % ncp, (te - tp) % ncs)
    n_in_seq = jnp.where(in_prompt, ncp, ncs)
    is_first = c_in_seq == jnp.where(fwd, 0, n_in_seq - 1)
    is_last = c_in_seq == jnp.where(fwd, n_in_seq - 1, 0)

    @pl.when(jnp.logical_and(is_first, in_prompt))
    def _():
        state_ref[...] = jnp.zeros_like(state_ref)

    @pl.when(jnp.logical_and(is_first, jnp.logical_not(in_prompt)))
    def _():
        for g in range(ng):
            state_ref[:, g * gp:(g + 1) * gp] = h0_ref[g * r:(g + 1) * r].reshape(gp, ns).T

    dt2 = dt_ref[...]
    da2 = dt2 * (-jnp.exp(alog_ref[...]))
    row = lax.broadcasted_iota(jnp.int32, (lc, lc), 0)
    col = lax.broadcasted_iota(jnp.int32, (lc, lc), 1)
    diff = jnp.where(fwd, row - col, col - row)
    mask = diff >= 0
    tri = jnp.where(mask, 1.0, 0.0).astype(f32)
    acs2 = jnp.dot(tri, da2, precision=HIGHEST, preferred_element_type=f32)
    acst2 = acs2.T
    dtt2 = dt2.T
    acs = jnp.where(fwd, acs2[:, :nh], acs2[:, nh:])
    acst = jnp.where(fwd, acst2[:nh], acst2[nh:])
    dtt = jnp.where(fwd, dtt2[:nh], dtt2[nh:])
    total = jnp.where(fwd, acst[:, lc - 1:lc], acst[:, 0:1])
    totb = jnp.broadcast_to(total, (nh, lc))
    acst_ref[...] = acst
    dtt_ref[...] = dtt
    wt_ref[...] = dtt * jnp.exp(totb - acst)
    cdt_ref[...] = jnp.exp(totb)
    for g in range(ng):
        acsc_ref[g] = acs[:, g * r:(g + 1) * r]
    pair = SLAB_LANES // hp
    first_lanes = lax.broadcasted_iota(jnp.int32, (1, SLAB_LANES), 1) < hp

    def group(g, carry):
        lo = pl.multiple_of(g * gp, gp)
        bg = bc_ref[:, pl.ds(pl.multiple_of(g * ns, ns), ns)]
        cg = bc_ref[:, pl.ds(pl.multiple_of((ng + g) * ns, ns), ns)]
        scores = lax.dot_general(cg, bg, (((1,), (1,)), ((), ())), preferred_element_type=f32)
        bt = bg.astype(f32).T
        cf = cg.astype(f32)
        cols = acsc_ref[g]
        hrow = pl.ds(pl.multiple_of(g * r, r), r)
        rows_a, rows_dt, rows_w, rows_cd = acst_ref[hrow, :], dtt_ref[hrow, :], wt_ref[hrow, :], cdt_ref[hrow, :]
        zero = jnp.zeros((ns, ns), bf16)
        glanes = pl.ds(lo, gp)
        x_g = xs_ref[:, glanes].astype(bf16)
        st_g = state_ref[:, glanes]
        ys, sts = [], []
        for k in range(r // pair):
            st = st_g[:, k * SLAB_LANES:(k + 1) * SLAB_LANES]
            rhs = jnp.concatenate([x_g[:, k * SLAB_LANES:(k + 1) * SLAB_LANES], st.astype(bf16)], axis=0)
            blocks = []
            for q in (pair * k, pair * k + 1):
                a_l = jnp.broadcast_to(cols[:, q:q + 1], (lc, lc))
                seg = a_l - rows_a[q:q + 1, :]
                mh = scores * jnp.where(mask, jnp.exp(seg), 0.0) * rows_dt[q:q + 1, :]
                ch = cf * jnp.exp(a_l)
                bw = bt * rows_w[q:q + 1, :]
                blocks.append(jnp.concatenate([mh.astype(bf16), ch.astype(bf16)], axis=1))
                blocks.append(jnp.concatenate([bw.astype(bf16), zero], axis=1))
            out = jnp.dot(jnp.concatenate(blocks, axis=0), rhs, preferred_element_type=f32)
            h0_rows, h1_rows = out[:lc + ns], out[lc + ns:]
            y = jnp.where(first_lanes, h0_rows[:lc], h1_rows[:lc])
            cs = jnp.where(first_lanes, h0_rows[lc:], h1_rows[lc:])
            cd = jnp.where(first_lanes, rows_cd[pair * k:pair * k + 1, :], rows_cd[pair * k + 1:pair * k + 2, :])
            sts.append(st * cd + cs)
            ys.append(y.astype(y_ref.dtype))
        state_ref[:, glanes] = jnp.concatenate(sts, axis=1)
        y_ref[:, glanes] = jnp.concatenate(ys, axis=1)
        return carry

    lax.fori_loop(0, ng, group, 0, unroll=4)

    @pl.when(jnp.logical_and(is_last, in_prompt))
    def _():
        for g in range(ng):
            st_ref[g * r:(g + 1) * r] = state_ref[:, g * gp:(g + 1) * gp].T.reshape(r, hp, ns)


def _ssd_gate_kernel(y_ref, xs_ref, z_ref, dx_ref, g_ref, o_ref):
    y = y_ref[0].astype(f32) + y_ref[1].astype(f32) + dx_ref[...] * xs_ref[...]
    y = y * _silu(z_ref[...])
    o_ref[...] = _rms(y, g_ref[...]).astype(o_ref.dtype)


def _ssd_mixer(h, j, h0, w_in, w_conv, b_conv, dt_bias, a_log, d_skip, norm_g, w_out, rows, n_prompt):
    m, d = h.shape
    nl = w_in.shape[0]
    n_seq_s, _, _, nh, hp, ns = h0.shape
    di = nh * hp
    conv_dim = w_conv.shape[-1]
    ng = (conv_dim - di) // (2 * ns)
    lc = SSM_CHUNK
    assert rows.seq_p % lc == 0 and rows.seq_s % lc == 0 and nh % ng == 0
    tm = rows.seq_s
    npt = rows.mp // tm

    tnz = _tile(di, 1024)
    tmz = _tile(rows.seq_s, 1024)
    z = _fused_matmul(h, w_in, j, [0], tm=tmz, tn=tnz, n_tiles=di // tnz, extras=[],
                      outs=[_out_tile(m, di, tmz, tnz, f32)], epilogue=_epi_plain, name="ssd_in_z")[0]
    tnx = _tile(di, 512)
    assert di % tnx == 0 and (2 * ng * ns) % tnx == 0
    cw = w_conv.shape[1]
    bcv = b_conv.reshape(nl, 1, conv_dim)
    conv_epi = functools.partial(_conv_silu_epilogue, chunk=rows.chunk, n_prompt_tiles=npt,
                                 seq_p=rows.seq_p, seq_s=rows.seq_s)

    def conv_part(col0, width_cols, dtype, name):
        nt = width_cols // tnx
        woff = (di + col0) // tnx
        coff = col0 // tnx
        return _fused_matmul(
            h, w_in, j, [woff], tm=tm, tn=tnx, n_tiles=nt,
            extras=[(w_conv, pl.BlockSpec((None, cw, tnx), lambda jj, i: (j, 0, coff + jj))),
                    (bcv, _bias_spec(j, tnx, coff))],
            outs=[_out_tile(m, width_cols, tm, tnx, dtype)], epilogue=conv_epi, chunk=rows.chunk, name=name)[0]

    xs = conv_part(0, di, f32, "ssd_in_x")
    bc = conv_part(di, 2 * ng * ns, bf16, "ssd_in_bc")
    ndt = 2 * nh
    dt = _fused_matmul(h, w_in, j, [(di + conv_dim) // ndt], tm=tmz, tn=ndt, n_tiles=1,
                       extras=[(dt_bias.reshape(nl, 1, ndt), _bias_spec(j, ndt))],
                       outs=[_out_tile(m, ndt, tmz, ndt, f32)], epilogue=_softplus_epilogue, name="ssd_in_dt")[0]

    tp, ts = rows.mp // lc, rows.ms // lc
    ncp, ncs = rows.seq_p // lc, rows.seq_s // lc
    nt = tp + ts
    assert lc == SLAB_LANES and ns == lc and 2 * hp == SLAB_LANES and (nh // ng) % 2 == 0 and ng % 2 == 0
    chunk = lambda dd, t: jnp.where(dd == 0, t, nt - 1 - t)
    rowmap = lambda dd, t: (chunk(dd, t), 0)
    y2, new_state = pl.pallas_call(
        functools.partial(_ssd_scan_kernel, nh=nh, hp=hp, ns=ns, ng=ng, tp=tp, ncp=ncp, ncs=ncs),
        grid=(2, nt),
        in_specs=[pl.BlockSpec((lc, di), rowmap),
                  pl.BlockSpec((lc, 2 * ng * ns), rowmap),
                  pl.BlockSpec((lc, ndt), rowmap),
                  pl.BlockSpec((None, 1, ndt), lambda dd, t: (j, 0, 0)),
                  pl.BlockSpec((None, None, None, nh, hp, ns),
                               lambda dd, t: (jnp.clip((chunk(dd, t) - tp) // ncs, 0, n_seq_s - 1), j, dd, 0, 0, 0))],
        out_specs=[pl.BlockSpec((None, lc, di), lambda dd, t: (dd, chunk(dd, t), 0)),
                   pl.BlockSpec((None, None, nh, hp, ns),
                                lambda dd, t: (jnp.clip(chunk(dd, t) // ncp, 0, n_prompt - 1), dd, 0, 0, 0))],
        out_shape=[jax.ShapeDtypeStruct((2, m, di), bf16),
                   jax.ShapeDtypeStruct((n_prompt, 2, nh, hp, ns), f32)],
        scratch_shapes=[pltpu.VMEM((ns, di), f32)] + [pltpu.VMEM((nh, lc), f32)] * 4
                       + [pltpu.VMEM((ng, lc, nh // ng), f32)],
        compiler_params=_params(2), name="ssd_scan",
    )(xs, bc, dt, a_log.reshape(nl, 1, ndt), h0)

    tg = _tile(rows.seq_s, 256)
    vrow = pl.BlockSpec((tg, di), lambda i: (i, 0))
    gated = pl.pallas_call(
        _ssd_gate_kernel, grid=(m // tg,),
        in_specs=[pl.BlockSpec((2, tg, di), lambda i: (0, i, 0)), vrow, vrow,
                  pl.BlockSpec((None, 1, di), lambda i: (j, 0, 0)),
                  pl.BlockSpec((None, 1, di), lambda i: (j, 0, 0))],
        out_specs=vrow, out_shape=jax.ShapeDtypeStruct((m, di), bf16),
        compiler_params=_params(1), name="ssd_gate",
    )(y2, xs, z, jnp.repeat(d_skip, hp, axis=1).reshape(nl, 1, di), norm_g.reshape(nl, 1, di))
    tmo, tno = _tile(m, 512), _tile(d, 512)
    out = _fused_matmul(gated, w_out, j, [0], tm=tmo, tn=tno, n_tiles=d // tno, extras=[],
                        outs=[_out_tile(m, d, tmo, tno, f32)], epilogue=_epi_plain, name="ssd_out")[0]
    return out, new_state


def _rope(x, cos, sin):
    n = x.shape[-1]
    lane = lax.broadcasted_iota(jnp.int32, x.shape, x.ndim - 1)
    partner = jnp.where((lane & 1) == 0, pltpu.roll(x, n - 1, axis=x.ndim - 1), pltpu.roll(x, 1, axis=x.ndim - 1))
    return x * cos + partner * sin


def _q_epilogue(accs, e_refs, o_refs, i, *, hd, scale):
    gain_ref, cos_ref, sin_ref = e_refs
    acc = accs[0]
    cos, sin, gain = cos_ref[...], sin_ref[...], gain_ref[...]
    for hh in range(acc.shape[1] // hd):
        q = _rope(_rms(acc[:, hh * hd:(hh + 1) * hd], gain), cos, sin) * scale
        o_refs[0][:, hh * hd:(hh + 1) * hd] = q.astype(o_refs[0].dtype)


def _kv_epilogue(accs, e_refs, o_refs, i, *, hd, nkv):
    gain_ref, cos_ref, sin_ref = e_refs
    acc = accs[0]
    cos, sin, gain = cos_ref[...], sin_ref[...], gain_ref[...]
    o_refs[1][...] = acc
    for hh in range(nkv):
        kn = _rms(acc[:, hh * hd:(hh + 1) * hd], gain)
        o_refs[1][:, hh * hd:(hh + 1) * hd] = kn
        o_refs[0][:, hh * hd:(hh + 1) * hd] = _rope(kn, cos, sin).astype(o_refs[0].dtype)
    o_refs[0][:, nkv * hd:] = acc[:, nkv * hd:].astype(o_refs[0].dtype)


def _attn_kernel(q_ref, k_ref, v_ref, *rest, group, hd, cached):
    if cached:
        ck_ref, cv_ref, o_ref = rest
        ck = ck_ref[...].astype(bf16)
        cv = cv_ref[...].astype(bf16)
    else:
        (o_ref,) = rest
    k = k_ref[...]
    v = v_ref[...]
    nt = (((1,), (1,)), ((), ()))
    for g in range(group):
        q = q_ref[:, g * hd:(g + 1) * hd]
        s = lax.dot_general(q, k, nt, preferred_element_type=f32)
        mx = jnp.max(s, axis=-1, keepdims=True)
        if cached:
            s2 = lax.dot_general(q, ck, nt, preferred_element_type=f32)
            mx = jnp.maximum(mx, jnp.max(s2, axis=-1, keepdims=True))
        p = jnp.exp(s - mx)
        den = jnp.sum(p, axis=-1, keepdims=True)
        o = jnp.dot(p.astype(bf16), v, preferred_element_type=f32)
        if cached:
            p2 = jnp.exp(s2 - mx)
            den = den + jnp.sum(p2, axis=-1, keepdims=True)
            o = o + jnp.dot(p2.astype(bf16), cv, preferred_element_type=f32)
        o_ref[:, g * hd:(g + 1) * hd] = (o / den).astype(o_ref.dtype)


def _attention(h, j, cache_k, cache_v, w_qkv, q_norm, k_norm, w_o, rows, n_prompt):
    m, d = h.shape
    nl = w_qkv.shape[0]
    hd = q_norm.shape[-1]
    n_seq_s, _, past, nkv, _ = cache_k.shape
    nq = w_o.shape[1] // hd
    group = nq // nkv
    assert rows.seq_s % GRID_W == 0
    pos = jnp.arange(rows.seq_s)
    inv = ROPE_THETA ** (-jnp.arange(hd // 4, dtype=f32) / (hd // 4))
    ang = jnp.concatenate([(pos // GRID_W).astype(f32)[:, None] * inv, (pos % GRID_W).astype(f32)[:, None] * inv], -1)
    cos_s = jnp.repeat(jnp.cos(ang), 2, axis=-1)
    sin_s = jnp.stack([-jnp.sin(ang), jnp.sin(ang)], axis=-1).reshape(rows.seq_s, hd)
    reps = rows.ms // rows.seq_s
    cos_t = jnp.concatenate([jnp.ones((rows.mp, hd), f32), jnp.tile(cos_s, (reps, 1))], 0)
    sin_t = jnp.concatenate([jnp.zeros((rows.mp, hd), f32), jnp.tile(sin_s, (reps, 1))], 0)

    tm = _tile(rows.seq_s, 1024)
    tab = pl.BlockSpec((tm, hd), lambda jj, i: (i, 0))
    gspec = pl.BlockSpec((None, 1, hd), lambda jj, i: (j, 0, 0))
    nqc, nkc = nq * hd, nkv * hd
    tnq = _tile(nqc, 1024)
    q = _fused_matmul(h, w_qkv, j, [0], tm=tm, tn=tnq, n_tiles=nqc // tnq,
                      extras=[(q_norm.reshape(nl, 1, hd), gspec), (cos_t, tab), (sin_t, tab)],
                      outs=[_out_tile(m, nqc, tm, tnq, bf16)],
                      epilogue=functools.partial(_q_epilogue, hd=hd, scale=hd ** -0.5), name="attn_q")[0]
    tnk = 2 * nkc
    assert nqc % tnk == 0
    kv, kv_f32 = _fused_matmul(h, w_qkv, j, [nqc // tnk], tm=tm, tn=tnk, n_tiles=1,
                               extras=[(k_norm.reshape(nl, 1, hd), gspec), (cos_t, tab), (sin_t, tab)],
                               outs=[_out_tile(m, tnk, tm, tnk, bf16), _out_tile(m, tnk, tm, tnk, f32)],
                               epilogue=functools.partial(_kv_epilogue, hd=hd, nkv=nkv), name="attn_kv")

    gw = group * hd
    sp = rows.seq_p
    o_p = pl.pallas_call(
        functools.partial(_attn_kernel, group=group, hd=hd, cached=False),
        grid=(n_prompt, nkv),
        in_specs=[pl.BlockSpec((sp, gw), lambda b, kh: (b, kh)),
                  pl.BlockSpec((sp, hd), lambda b, kh: (b, kh)),
                  pl.BlockSpec((sp, hd), lambda b, kh: (b, nkv + kh))],
        out_specs=pl.BlockSpec((sp, gw), lambda b, kh: (b, kh)),
        out_shape=jax.ShapeDtypeStruct((rows.mp, nqc), bf16),
        compiler_params=_params(2), name="attn_prompt",
    )(q, kv, kv)
    ss = rows.seq_s
    tq = _tile(ss, 512)
    nqb = ss // tq
    pb = rows.mp // ss
    pbq = rows.mp // tq
    ck = cache_k[:, j].reshape(n_seq_s, past, nkc)
    cv = cache_v[:, j].reshape(n_seq_s, past, nkc)
    o_s = pl.pallas_call(
        functools.partial(_attn_kernel, group=group, hd=hd, cached=True),
        grid=(n_seq_s, nkv, nqb),
        in_specs=[pl.BlockSpec((tq, gw), lambda b, kh, qi: (pbq + b * nqb + qi, kh)),
                  pl.BlockSpec((ss, hd), lambda b, kh, qi: (pb + b, kh)),
                  pl.BlockSpec((ss, hd), lambda b, kh, qi: (pb + b, nkv + kh)),
                  pl.BlockSpec((None, past, hd), lambda b, kh, qi: (b, 0, kh)),
                  pl.BlockSpec((None, past, hd), lambda b, kh, qi: (b, 0, kh))],
        out_specs=pl.BlockSpec((tq, gw), lambda b, kh, qi: (b * nqb + qi, kh)),
        out_shape=jax.ShapeDtypeStruct((rows.ms, nqc), bf16),
        compiler_params=_params(3), name="attn_sample",
    )(q, kv, kv, ck, cv)
    o = jnp.concatenate([o_p, o_s], axis=0)
    tmo, tno = _tile(m, 1024), _tile(d, 1024)
    out = _fused_matmul(o, w_o, j, [0], tm=tmo, tn=tno, n_tiles=d // tno, extras=[],
                        outs=[_out_tile(m, d, tmo, tno, f32)], epilogue=_epi_plain, name="attn_o")[0]
    new_k = kv_f32[:rows.mp, :nkc].reshape(n_prompt, sp, nkv, hd)
    new_v = kv_f32[:rows.mp, nkc:].reshape(n_prompt, sp, nkv, hd)
    return out, new_k, new_v


def kernel(x_prompt, x_sample, c, state_ssd, cache_k, cache_v, c_ctx, w_mod, b_mod, norm_pre, norm_post, cv_w_pw1, cv_b_pw1, cv_w_dw, cv_b_dw, cv_ln_g, cv_ln_b, cv_w_pw2, cv_b_pw2, ssd_w_in, ssd_w_conv, ssd_b_conv, ssd_dt_bias, ssd_a_log, ssd_d, ssd_norm_g, ssd_w_out, attn_w_qkv, attn_q_norm, attn_k_norm, attn_w_o, ffn_w_up, ffn_w_dw, ffn_b_dw, ffn_w_down):
    nb, sp, d = x_prompt.shape
    ndb, ss, _ = x_sample.shape
    depth = w_mod.shape[0]
    rows = _Rows(nb * sp, ndb * ss, sp, ss)
    assert sp & (sp - 1) == 0 and ss & (ss - 1) == 0, "sequence lengths must be powers of two"
    assert rows.mp % ss == 0 and 1 + ndb <= COND_ROWS

    y = (x_prompt.reshape(rows.mp, d), x_sample.reshape(rows.ms, d))
    cond = jnp.concatenate([c_ctx[None, :], c, jnp.zeros((COND_ROWS - 1 - ndb, d), f32)], axis=0)
    mod = _modulation(cond, w_mod, b_mod).reshape(depth * COND_ROWS, 6, d)
    gpre = norm_pre.reshape(depth * 2, 1, d)
    gpost = norm_post.reshape(depth * 2, 1, d)
    h = _prenorm(y, gpre, 0, mod, 0, 1, 0, rows)
    new_ssd, new_k, new_v = [], [], []
    for i in range(depth):
        kind, j = i % N_MIXERS, i // N_MIXERS
        if kind == 0:
            o = _conformer(h, j, cv_w_pw1, cv_b_pw1, cv_w_dw, cv_b_dw, cv_ln_g, cv_ln_b, cv_w_pw2, cv_b_pw2, rows)
        elif kind == 1:
            o, st = _ssd_mixer(h, j, state_ssd, ssd_w_in, ssd_w_conv, ssd_b_conv, ssd_dt_bias, ssd_a_log,
                               ssd_d, ssd_norm_g, ssd_w_out, rows, nb)
            new_ssd.append(st)
        else:
            o, k_new, v_new = _attention(h, j, cache_k, cache_v, attn_w_qkv, attn_q_norm, attn_k_norm,
                                         attn_w_o, rows, nb)
            new_k.append(k_new)
            new_v.append(v_new)
        y, h = _resid(y, o, gpost, 2 * i, mod, i, 2, rows, nxt=(gpre, 2 * i + 1, i, 4, 3))
        o = _conv_ffn(h, i, ffn_w_up, ffn_w_dw, ffn_b_dw, ffn_w_down, rows)
        if i + 1 < depth:
            y, h = _resid(y, o, gpost, 2 * i + 1, mod, i, 5, rows, nxt=(gpre, 2 * i + 2, i + 1, 1, 0))
        else:
            yp, ys = _resid(y, o, gpost, 2 * i + 1, mod, i, 5, rows, split_out=True)
    yp = yp.reshape(nb, sp, d)
    ys = ys.reshape(ndb, ss, d)
    return (yp, ys, jnp.stack(new_ssd, axis=1), jnp.stack(new_k, axis=1), jnp.stack(new_v, axis=1))
```

```python
import functools
import math

import jax
import jax.numpy as jnp
from jax import lax
from jax.experimental import pallas as pl
from jax.experimental.pallas import tpu as pltpu

f32 = jnp.float32
bf16 = jnp.bfloat16
BRANCH_DTYPE = bf16

EPS = 1e-6
GRID_W = 64
ROPE_THETA = 10000.0
SSM_CHUNK = 128
N_MIXERS = 3
V7X_VMEM_LIMIT_BYTES = 56 * 1024 * 1024
COND_ROWS = 8
ROW_HALO = 8
ROW_CHUNK = 256
CONV_BLOCK = 256
SLAB_LANES = 128
EPILOGUE_LAG = 2


def _params(n_axes, vmem=V7X_VMEM_LIMIT_BYTES):
    return pltpu.CompilerParams(dimension_semantics=("arbitrary",) * n_axes,
                                vmem_limit_bytes=vmem)


def _tile(dim, pref):
    return pref if dim % pref == 0 else dim


def _sigmoid(x):
    return 1.0 / (1.0 + jnp.exp(-x))


def _silu(x):
    return x * _sigmoid(x)


def _rms(x, g):
    return x * lax.rsqrt(jnp.mean(x * x, axis=-1, keepdims=True) + EPS) * g


def _mod_kernel(c_ref, w_ref, b_ref, o_ref):
    s = _silu(c_ref[...]).astype(bf16)
    o_ref[...] = jnp.dot(s, w_ref[...].astype(bf16), preferred_element_type=f32) + b_ref[...]


def _modulation(cond, w_mod, b_mod):
    depth, d, n = w_mod.shape
    tn = _tile(n, 1024)
    return pl.pallas_call(
        _mod_kernel,
        grid=(depth, n // tn),
        in_specs=[pl.BlockSpec((COND_ROWS, d), lambda l, j: (0, 0)),
                  pl.BlockSpec((None, d, tn), lambda l, j: (l, 0, j)),
                  pl.BlockSpec((None, 1, tn), lambda l, j: (l, 0, j))],
        out_specs=pl.BlockSpec((None, COND_ROWS, tn), lambda l, j: (l, 0, j)),
        out_shape=jax.ShapeDtypeStruct((depth, COND_ROWS, n), f32),
        compiler_params=_params(2), name="modulation",
    )(cond, w_mod, b_mod.reshape(depth, 1, n))


def _prenorm_kernel(yp_ref, ys_ref, g_ref, mod_ref, h_ref, *, sc, sh, n_prompt_tiles):
    y = jnp.where(pl.program_id(0) < n_prompt_tiles, yp_ref[...], ys_ref[...])
    yn = _rms(y, g_ref[...])
    h_ref[...] = (yn * (1.0 + mod_ref[sc:sc + 1, :]) + mod_ref[sh:sh + 1, :]).astype(h_ref.dtype)


def _resid_kernel(*refs, ga, nxt, split_in, split_out, n_prompt_tiles):
    i = pl.program_id(0)
    if split_in:
        yp_ref, ys_ref, o_ref, gpost_ref, moda_ref, *rest = refs
        y_in = jnp.where(i < n_prompt_tiles, yp_ref[...], ys_ref[...])
    else:
        y_ref, o_ref, gpost_ref, moda_ref, *rest = refs
        y_in = y_ref[...]
    y = y_in + moda_ref[ga:ga + 1, :] * _rms(o_ref[...].astype(f32), gpost_ref[...])
    if split_out:
        outp_ref, outs_ref = rest

        @pl.when(i < n_prompt_tiles)
        def _():
            outp_ref[...] = y

        @pl.when(i >= n_prompt_tiles)
        def _():
            outs_ref[...] = y
    elif nxt is None:
        (ynew_ref,) = rest
        ynew_ref[...] = y
    else:
        gpre_ref, modb_ref, ynew_ref, h_ref = rest
        sc, sh = nxt
        ynew_ref[...] = y
        yn = _rms(y, gpre_ref[...])
        h_ref[...] = (yn * (1.0 + modb_ref[sc:sc + 1, :]) + modb_ref[sh:sh + 1, :]).astype(h_ref.dtype)


class _Rows:
    def __init__(self, mp, ms, seq_p, seq_s):
        self.mp, self.ms, self.m = mp, ms, mp + ms
        self.seq_p, self.seq_s = seq_p, seq_s
        self.chunk = _tile(seq_p, ROW_CHUNK)
        assert seq_p % self.chunk == 0 and seq_s % self.chunk == 0

    def cond_of_tile(self, tm):
        assert self.mp % tm == 0 and self.seq_s % tm == 0
        npt, per = self.mp // tm, self.seq_s // tm
        return lambda i: jnp.where(i < npt, 0, 1 + (i - npt) // per)


def _split_specs(rows, tm, d):
    npt = rows.mp // tm
    return [pl.BlockSpec((tm, d), lambda i: (jnp.minimum(i, npt - 1), 0)),
            pl.BlockSpec((tm, d), lambda i: (jnp.maximum(i - npt, 0), 0))]


def _prenorm(y_pair, g_all, g_idx, mod, layer, sc, sh, rows):
    d = y_pair[0].shape[1]
    m = rows.m
    tm = _tile(rows.seq_s, 256)
    cond = rows.cond_of_tile(tm)
    return pl.pallas_call(
        functools.partial(_prenorm_kernel, sc=sc, sh=sh, n_prompt_tiles=rows.mp // tm),
        grid=(m // tm,),
        in_specs=_split_specs(rows, tm, d)
                 + [pl.BlockSpec((None, 1, d), lambda i: (g_idx, 0, 0)),
                    pl.BlockSpec((None, 6, d), lambda i: (layer * COND_ROWS + cond(i), 0, 0))],
        out_specs=pl.BlockSpec((tm, d), lambda i: (i, 0)),
        out_shape=jax.ShapeDtypeStruct((m, d), bf16),
        compiler_params=_params(1), name="prenorm",
    )(*y_pair, g_all, mod)


def _resid(y, o, gpost_all, gpost_idx, mod, layer, ga, rows, nxt=None, split_out=False):
    m, d = o.shape
    tm = _tile(rows.seq_s, 256)
    cond = rows.cond_of_tile(tm)
    row_spec = pl.BlockSpec((tm, d), lambda i: (i, 0))
    split_in = isinstance(y, tuple)
    in_specs = (_split_specs(rows, tm, d) if split_in else [row_spec]) + [
        row_spec,
        pl.BlockSpec((None, 1, d), lambda i: (gpost_idx, 0, 0)),
        pl.BlockSpec((None, 6, d), lambda i: (layer * COND_ROWS + cond(i), 0, 0))]
    args = (list(y) if split_in else [y]) + [o, gpost_all, mod]
    if split_out:
        assert nxt is None
        out_specs = _split_specs(rows, tm, d)
        out_shape = [jax.ShapeDtypeStruct((rows.mp, d), f32), jax.ShapeDtypeStruct((rows.ms, d), f32)]
        kn = None
    elif nxt is None:
        out_specs, out_shape, kn = row_spec, jax.ShapeDtypeStruct((m, d), f32), None
    else:
        gpre_all, gpre_idx, nlayer, sc, sh = nxt
        in_specs += [pl.BlockSpec((None, 1, d), lambda i: (gpre_idx, 0, 0)),
                     pl.BlockSpec((None, 6, d), lambda i: (nlayer * COND_ROWS + cond(i), 0, 0))]
        args += [gpre_all, mod]
        out_specs = [row_spec, row_spec]
        out_shape = [jax.ShapeDtypeStruct((m, d), f32), jax.ShapeDtypeStruct((m, d), bf16)]
        kn = (sc, sh)
    return pl.pallas_call(
        functools.partial(_resid_kernel, ga=ga, nxt=kn, split_in=split_in, split_out=split_out,
                          n_prompt_tiles=rows.mp // tm),
        grid=(m // tm,), in_specs=in_specs, out_specs=out_specs, out_shape=out_shape,
        compiler_params=_params(1), name="resid",
    )(*args)


def _fmm_kernel(*refs, n_slab, n_extra, n_out, epilogue, chunk):
    x_ref = refs[0]
    w_refs = refs[1:1 + n_slab]
    e_refs = refs[1 + n_slab:1 + n_slab + n_extra]
    o_refs = refs[1 + n_slab + n_extra:1 + n_slab + n_extra + n_out]
    wb_refs = refs[1 + n_slab + n_extra + n_out:1 + 2 * n_slab + n_extra + n_out]
    u_refs = refs[1 + 2 * n_slab + n_extra + n_out:]
    i = pl.program_id(1)

    @pl.when(i == 0)
    def _():
        for w_ref, wb_ref in zip(w_refs, wb_refs):
            wb_ref[...] = w_ref[...].astype(bf16)

    if chunk is None:
        x = x_ref[...]
        accs = [jnp.dot(x, wb_ref[...], preferred_element_type=f32) for wb_ref in wb_refs]
        epilogue(accs, e_refs, o_refs, i)
        return

    tm = x_ref.shape[0]
    n_chunk = tm // chunk
    per_slab = [u_refs[s * n_chunk:(s + 1) * n_chunk] for s in range(n_slab)]
    zeros = jnp.zeros((ROW_HALO, u_refs[0].shape[1]), f32)
    for bufs in per_slab:
        bufs[0][0:ROW_HALO, :] = zeros
        bufs[-1][ROW_HALO + chunk:, :] = zeros
    for c in range(n_chunk + EPILOGUE_LAG):
        if c < n_chunk:
            xc = x_ref[c * chunk:(c + 1) * chunk, :]
            for wb_ref, bufs in zip(wb_refs, per_slab):
                prod = jnp.dot(xc, wb_ref[...], preferred_element_type=f32)
                bufs[c][ROW_HALO:ROW_HALO + chunk, :] = prod
                if c > 0:
                    bufs[c - 1][ROW_HALO + chunk:, :] = prod[0:ROW_HALO]
                if c + 1 < n_chunk:
                    bufs[c + 1][0:ROW_HALO, :] = prod[chunk - ROW_HALO:]
        if c >= EPILOGUE_LAG:
            done = c - EPILOGUE_LAG
            epilogue(done, [bufs[done] for bufs in per_slab], e_refs, o_refs, i)


def _fused_matmul(x, w_all, layer, slab_offs, *, tm, tn, n_tiles, extras, outs, epilogue, name, chunk=None):
    m, k = x.shape
    in_specs = [pl.BlockSpec((tm, k), lambda j, i: (i, 0))]
    args = [x]
    for off in slab_offs:
        in_specs.append(pl.BlockSpec((None, k, tn), lambda j, i, off=off: (layer, 0, off + j)))
        args.append(w_all)
    for arr, spec in extras:
        in_specs.append(spec)
        args.append(arr)
    scratch = [pltpu.VMEM((k, tn), bf16) for _ in slab_offs]
    if chunk is not None:
        assert tm % chunk == 0
        scratch += [pltpu.VMEM((chunk + 2 * ROW_HALO, tn), f32)] * (len(slab_offs) * (tm // chunk))
    return pl.pallas_call(
        functools.partial(_fmm_kernel, n_slab=len(slab_offs), n_extra=len(extras),
                          n_out=len(outs), epilogue=epilogue, chunk=chunk),
        grid=(n_tiles, m // tm),
        in_specs=in_specs,
        out_specs=[spec for _, spec in outs],
        out_shape=[sds for sds, _ in outs],
        scratch_shapes=scratch,
        compiler_params=_params(2), name=name,
    )(*args)


def _bias_spec(layer, tn, off=0):
    return pl.BlockSpec((None, 1, tn), lambda j, i: (layer, 0, off + j))


def _out_tile(m, n, tm, tn, dtype):
    return (jax.ShapeDtypeStruct((m, n), dtype), pl.BlockSpec((tm, tn), lambda j, i: (i, j)))


def _epi_plain(accs, e_refs, o_refs, i):
    o_refs[0][...] = accs[0].astype(o_refs[0].dtype)


def _epi_bias(accs, e_refs, o_refs, i):
    o_refs[0][...] = (accs[0] + e_refs[0][...]).astype(o_refs[0].dtype)


class _ChunkConv:
    def __init__(self, u_ref, w, bias, c, chunk, seq):
        self.u_ref, self.chunk = u_ref, chunk
        self.width = w.shape[0]
        self.left = (self.width - 1) // 2
        assert self.left <= ROW_HALO and self.width - 1 - self.left <= ROW_HALO
        assert chunk % CONV_BLOCK == 0 and CONV_BLOCK >= 2 * ROW_HALO
        tn = u_ref.shape[1]
        self.starts_seq = ((c * chunk) & (seq - 1)) == 0
        self.ends_seq = (((c + 1) * chunk) & (seq - 1)) == 0
        self.w_tile = [jnp.broadcast_to(w[k:k + 1, :], (ROW_HALO, tn)) for k in range(self.width)]
        self.w_block = [jnp.tile(wt, (CONV_BLOCK // ROW_HALO, 1)) for wt in self.w_tile]
        self.b_tile = jnp.broadcast_to(bias, (ROW_HALO, tn))
        self.b_block = jnp.tile(self.b_tile, (CONV_BLOCK // ROW_HALO, 1))

    def _taps(self, r0, n_rows, masked):
        w_rows = self.w_tile if n_rows == ROW_HALO else self.w_block
        acc = self.b_tile if n_rows == ROW_HALO else self.b_block
        far = self.chunk + ROW_HALO
        if masked:
            row = lax.broadcasted_iota(jnp.int32, acc.shape, 0) + r0
        for k in range(self.width):
            d = k - self.left
            v = self.u_ref[pl.ds(ROW_HALO + r0 + d, n_rows), :]
            if masked and d < 0:
                v = jnp.where(row >= jnp.where(self.starts_seq, -d, -far), v, 0.0)
            if masked and d > 0:
                v = jnp.where(row < jnp.where(self.ends_seq, self.chunk - d, far), v, 0.0)
            acc = acc + v * w_rows[k]
        return acc

    def block(self, r0):
        out = self._taps(r0, CONV_BLOCK, False)
        if r0 == 0:
            out = jnp.concatenate([self._taps(0, ROW_HALO, True), out[ROW_HALO:]], axis=0)
        if r0 + CONV_BLOCK == self.chunk:
            tail = self._taps(self.chunk - ROW_HALO, ROW_HALO, True)
            out = jnp.concatenate([out[:CONV_BLOCK - ROW_HALO], tail], axis=0)
        return out


def _ffn_up_epilogue(c, u_refs, e_refs, o_refs, i, *, chunk, n_prompt_tiles, seq_p, seq_s):
    wa_ref, wg_ref, ba_ref, bg_ref = e_refs
    seq = jnp.where(i < n_prompt_tiles, seq_p, seq_s)
    conv_a = _ChunkConv(u_refs[0], wa_ref[...], ba_ref[...], c, chunk, seq)
    conv_g = _ChunkConv(u_refs[1], wg_ref[...], bg_ref[...], c, chunk, seq)
    for r0 in range(0, chunk, CONV_BLOCK):
        a, g = conv_a.block(r0), conv_g.block(r0)
        o_refs[0][c * chunk + r0:c * chunk + r0 + CONV_BLOCK, :] = (_silu(g) * a).astype(o_refs[0].dtype)


def _conv_ffn(h, layer, w_up, w_dw, b_dw, w_down, rows):
    m, d = h.shape
    dff = w_down.shape[1]
    depth = w_up.shape[0]
    tm = rows.seq_s
    tn = _tile(dff, 512)
    nt = dff // tn
    width = w_dw.shape[1]
    b3 = b_dw.reshape(depth, 1, 2 * dff)
    dw_spec = lambda off: pl.BlockSpec((None, width, tn), lambda j, i: (layer, 0, off + j))
    act = _fused_matmul(
        h, w_up, layer, [0, nt], tm=tm, tn=tn, n_tiles=nt,
        extras=[(w_dw, dw_spec(0)), (w_dw, dw_spec(nt)), (b3, _bias_spec(layer, tn)), (b3, _bias_spec(layer, tn, nt))],
        outs=[_out_tile(m, dff, tm, tn, bf16)], chunk=rows.chunk,
        epilogue=functools.partial(_ffn_up_epilogue, chunk=rows.chunk, n_prompt_tiles=rows.mp // tm,
                                   seq_p=rows.seq_p, seq_s=rows.seq_s),
        name="ffn_up")[0]
    tm2, tn2 = _tile(m, 512), _tile(d, 512)
    return _fused_matmul(act, w_down, layer, [0], tm=tm2, tn=tn2, n_tiles=d // tn2, extras=[],
                         outs=[_out_tile(m, d, tm2, tn2, BRANCH_DTYPE)], epilogue=_epi_plain, name="ffn_down")[0]


def _glu_epilogue(accs, e_refs, o_refs, i):
    a = accs[0] + e_refs[0][...]
    g = accs[1] + e_refs[1][...]
    o_refs[0][...] = (a * _sigmoid(g)).astype(o_refs[0].dtype)


CONV_HALO = 16
CONV_LANES = 128
CONV_ROWS = 128
LN_ROWS = 32


def _cv_mid_kernel(u_ref, prev_ref, next_ref, w_ref, b_ref, g_ref, beta_ref, o_ref, pad_ref, conv_ref,
                   *, blocks_p, per_seq_p, per_seq_s):
    i = pl.program_id(0)
    rb, d = u_ref.shape
    width = w_ref.shape[0]
    left = (width - 1) // 2
    per = jnp.where(i < blocks_p, per_seq_p, per_seq_s)
    k_in_seq = jnp.where(i < blocks_p, i, i - blocks_p) % per
    first = k_in_seq == 0
    last = k_in_seq == per - 1
    pad_ref[0:CONV_HALO, :] = jnp.where(first, 0.0, prev_ref[...])
    pad_ref[CONV_HALO:CONV_HALO + rb, :] = u_ref[...]
    pad_ref[CONV_HALO + rb:CONV_HALO + rb + CONV_HALO, :] = jnp.where(last, 0.0, next_ref[...])

    def lane_chunk(cidx, carry):
        lo = pl.multiple_of(cidx * CONV_LANES, CONV_LANES)
        wc = w_ref[:, pl.ds(lo, CONV_LANES)]
        for r0 in range(0, rb, CONV_ROWS):
            partial = []
            for phase in range(ROW_HALO):
                ks = [k for k in range(width) if (CONV_HALO - left + k) % ROW_HALO == phase]
                if not ks:
                    continue
                span = max(CONV_HALO - left + k for k in ks) - phase
                shifted = pad_ref[pl.ds(r0 + phase, CONV_ROWS + span), pl.ds(lo, CONV_LANES)]
                acc = None
                for k in ks:
                    off = CONV_HALO - left + k - phase
                    term = shifted[off:off + CONV_ROWS] * wc[k:k + 1, :]
                    acc = term if acc is None else acc + term
                partial.append(acc)
            total = partial[0]
            for acc in partial[1:]:
                total = total + acc
            conv_ref[pl.ds(r0, CONV_ROWS), pl.ds(lo, CONV_LANES)] = total
        return carry

    lax.fori_loop(0, d // CONV_LANES, lane_chunk, 0)

    def norm_rows(ridx, carry):
        rs = pl.ds(pl.multiple_of(ridx * LN_ROWS, LN_ROWS), LN_ROWS)
        u = conv_ref[rs, :] + b_ref[...]
        mu = jnp.mean(u, axis=-1, keepdims=True)
        uc = u - mu
        v = uc * lax.rsqrt(jnp.mean(uc * uc, axis=-1, keepdims=True) + EPS) * g_ref[...] + beta_ref[...]
        o_ref[rs, :] = _silu(v).astype(o_ref.dtype)
        return carry

    lax.fori_loop(0, rb // LN_ROWS, norm_rows, 0, unroll=2)


def _conformer(h, j, w_pw1, b_pw1, w_dw, b_dw, ln_g, ln_b, w_pw2, b_pw2, rows):
    m, d = h.shape
    nl = w_pw1.shape[0]
    tm, tn = _tile(rows.seq_s, 1024), _tile(d, 512)
    nt = d // tn
    b1 = b_pw1.reshape(nl, 1, 2 * d)
    glu = _fused_matmul(h, w_pw1, j, [0, nt], tm=tm, tn=tn, n_tiles=nt,
                        extras=[(b1, _bias_spec(j, tn)), (b1, _bias_spec(j, tn, nt))],
                        outs=[_out_tile(m, d, tm, tn, f32)], epilogue=_glu_epilogue, name="cv_pw1")[0]
    width = w_dw.shape[1]
    assert (width - 1) // 2 <= CONV_HALO
    rb = _tile(rows.seq_p, 256)
    assert rows.seq_p % rb == 0 and rows.seq_s % rb == 0 and rb % CONV_ROWS == 0 and d % CONV_LANES == 0
    hb = rb // CONV_HALO
    n_halo = m // CONV_HALO
    vec = lambda a: a.reshape(nl, 1, d)
    vspec = pl.BlockSpec((None, 1, d), lambda i: (j, 0, 0))
    mid = pl.pallas_call(
        functools.partial(_cv_mid_kernel, blocks_p=rows.mp // rb, per_seq_p=rows.seq_p // rb,
                          per_seq_s=rows.seq_s // rb),
        grid=(m // rb,),
        in_specs=[pl.BlockSpec((rb, d), lambda i: (i, 0)),
                  pl.BlockSpec((CONV_HALO, d), lambda i: (jnp.maximum(i * hb - 1, 0), 0)),
                  pl.BlockSpec((CONV_HALO, d), lambda i: (jnp.minimum((i + 1) * hb, n_halo - 1), 0)),
                  pl.BlockSpec((None, width, d), lambda i: (j, 0, 0)),
                  vspec, vspec, vspec],
        out_specs=pl.BlockSpec((rb, d), lambda i: (i, 0)),
        out_shape=jax.ShapeDtypeStruct((m, d), bf16),
        scratch_shapes=[pltpu.VMEM((rb + 2 * CONV_HALO, d), f32), pltpu.VMEM((rb, d), f32)],
        compiler_params=_params(1), name="cv_mid",
    )(glu, glu, glu, w_dw, vec(b_dw), vec(ln_g), vec(ln_b))
    tm2, tn2 = _tile(m, 1024), _tile(d, 1024)
    return _fused_matmul(mid, w_pw2, j, [0], tm=tm2, tn=tn2, n_tiles=d // tn2,
                         extras=[(vec(b_pw2), _bias_spec(j, tn2))],
                         outs=[_out_tile(m, d, tm2, tn2, BRANCH_DTYPE)], epilogue=_epi_bias, name="cv_pw2")[0]


def _conv_silu_epilogue(c, u_refs, e_refs, o_refs, i, *, chunk, n_prompt_tiles, seq_p, seq_s):
    w_ref, b_ref = e_refs
    seq = jnp.where(i < n_prompt_tiles, seq_p, seq_s)
    conv = _ChunkConv(u_refs[0], w_ref[...], b_ref[...], c, chunk, seq)
    for r0 in range(0, chunk, CONV_BLOCK):
        o_refs[0][c * chunk + r0:c * chunk + r0 + CONV_BLOCK, :] = _silu(conv.block(r0)).astype(o_refs[0].dtype)


def _softplus_epilogue(accs, e_refs, o_refs, i):
    x = accs[0] + e_refs[0][...]
    o_refs[0][...] = jnp.maximum(x, 0.0) + jnp.log1p(jnp.exp(-jnp.abs(x)))


def _ssd_scan_kernel(xs_ref, bc_ref, dt_ref, alog_ref, h0_ref, y_ref, st_ref,
                     state_ref, acst_ref, dtt_ref, wt_ref, cdt_ref, acsc_ref,
                     *, nh, hp, ns, ng, tp, ncp, ncs):
    d = pl.program_id(0)
    t = pl.program_id(1)
    nt = pl.num_programs(1)
    lc = xs_ref.shape[0]
    r = nh // ng
    gp = r * hp
    fwd = d == 0
    te = jnp.where(fwd, t, nt - 1 - t)
    in_prompt = te < tp
    c_in_seq = jnp.where(in_prompt, te % ncp, (te - tp) % ncs)
    n_in_seq = jnp.where(in_prompt, ncp, ncs)
    is_first = c_in_seq == jnp.where(fwd, 0, n_in_seq - 1)
    is_last = c_in_seq == jnp.where(fwd, n_in_seq - 1, 0)

    @pl.when(jnp.logical_and(is_first, in_prompt))
    def _():
        state_ref[...] = jnp.zeros_like(state_ref)

    @pl.when(jnp.logical_and(is_first, jnp.logical_not(in_prompt)))
    def _():
        for g in range(ng):
            state_ref[:, g * gp:(g + 1) * gp] = h0_ref[g * r:(g + 1) * r].reshape(gp, ns).T

    dt2 = dt_ref[...]
    da2 = dt2 * (-jnp.exp(alog_ref[...]))
    row = lax.broadcasted_iota(jnp.int32, (lc, lc), 0)
    col = lax.broadcasted_iota(jnp.int32, (lc, lc), 1)
    diff = jnp.where(fwd, row - col, col - row)
    mask = diff >= 0
    tri = jnp.where(mask, 1.0, 0.0).astype(bf16)
    hi = da2.astype(bf16)
    rest = da2 - hi.astype(f32)
    mid = rest.astype(bf16)
    low = (rest - mid.astype(f32)).astype(bf16)
    acs2 = (jnp.dot(tri, hi, preferred_element_type=f32) + jnp.dot(tri, mid, preferred_element_type=f32)
            + jnp.dot(tri, low, preferred_element_type=f32))
    acst2 = acs2.T
    dtt2 = dt2.T
    acs = jnp.where(fwd, acs2[:, :nh], acs2[:, nh:])
    acst = jnp.where(fwd, acst2[:nh], acst2[nh:])
    dtt = jnp.where(fwd, dtt2[:nh], dtt2[nh:])
    total = jnp.where(fwd, acst[:, lc - 1:lc], acst[:, 0:1])
    totb = jnp.broadcast_to(total, (nh, lc))
    acst_ref[...] = acst
    dtt_ref[...] = dtt
    wt_ref[...] = dtt * jnp.exp(totb - acst)
    cdt_ref[...] = jnp.exp(totb)
    for g in range(ng):
        acsc_ref[g] = acs[:, g * r:(g + 1) * r]
    pair = SLAB_LANES // hp
    first_lanes = lax.broadcasted_iota(jnp.int32, (1, SLAB_LANES), 1) < hp

    def group(g, carry):
        lo = pl.multiple_of(g * gp, gp)
        bg = bc_ref[:, pl.ds(pl.multiple_of(g * ns, ns), ns)]
        cg = bc_ref[:, pl.ds(pl.multiple_of((ng + g) * ns, ns), ns)]
        scores = lax.dot_general(cg, bg, (((1,), (1,)), ((), ())), preferred_element_type=f32)
        bt = bg.astype(f32).T
        cf = cg.astype(f32)
        cols = acsc_ref[g]
        hrow = pl.ds(pl.multiple_of(g * r, r), r)
        rows_a, rows_dt, rows_w, rows_cd = acst_ref[hrow, :], dtt_ref[hrow, :], wt_ref[hrow, :], cdt_ref[hrow, :]
        glanes = pl.ds(lo, gp)
        x_g = xs_ref[:, glanes].astype(bf16)
        st_g = state_ref[:, glanes]
        ys, sts = [], []
        for k in range(r // pair):
            st = st_g[:, k * SLAB_LANES:(k + 1) * SLAB_LANES]
            x_k = x_g[:, k * SLAB_LANES:(k + 1) * SLAB_LANES]
            tops, bws = [], []
            for q in (pair * k, pair * k + 1):
                a_l = jnp.broadcast_to(cols[:, q:q + 1], (lc, lc))
                seg = a_l - rows_a[q:q + 1, :]
                mh = scores * jnp.where(mask, jnp.exp(seg), 0.0) * rows_dt[q:q + 1, :]
                ch = cf * jnp.exp(a_l)
                tops.append(jnp.concatenate([mh.astype(bf16), ch.astype(bf16)], axis=1))
                bws.append((bt * rows_w[q:q + 1, :]).astype(bf16))
            out = jnp.dot(jnp.concatenate(tops, axis=0), jnp.concatenate([x_k, st.astype(bf16)], axis=0),
                          preferred_element_type=f32)
            y = jnp.where(first_lanes, out[:lc], out[lc:])
            x_0 = jnp.where(first_lanes, x_k, jnp.zeros_like(x_k))
            x_1 = jnp.where(first_lanes, jnp.zeros_like(x_k), x_k)
            cs = jnp.dot(jnp.concatenate(bws, axis=1), jnp.concatenate([x_0, x_1], axis=0),
                         preferred_element_type=f32)
            cd = jnp.where(first_lanes, rows_cd[pair * k:pair * k + 1, :], rows_cd[pair * k + 1:pair * k + 2, :])
            sts.append(st * cd + cs)
            ys.append(y.astype(y_ref.dtype))
        state_ref[:, glanes] = jnp.concatenate(sts, axis=1)
        y_ref[:, glanes] = jnp.concatenate(ys, axis=1)
        return carry

    lax.fori_loop(0, ng, group, 0, unroll=4)

    @pl.when(jnp.logical_and(is_last, in_prompt))
    def _():
        for g in range(ng):
            st_ref[g * r:(g + 1) * r] = state_ref[:, g * gp:(g + 1) * gp].T.reshape(r, hp, ns)


def _ssd_gate_kernel(y_ref, xs_ref, z_ref, dx_ref, g_ref, o_ref):
    y = y_ref[0].astype(f32) + y_ref[1].astype(f32) + dx_ref[...] * xs_ref[...].astype(f32)
    y = y * _silu(z_ref[...].astype(f32))
    o_ref[...] = _rms(y, g_ref[...]).astype(o_ref.dtype)


def _ssd_mixer(h, j, h0, w_in, w_conv, b_conv, dt_bias, a_log, d_skip, norm_g, w_out, rows, n_prompt):
    m, d = h.shape
    nl = w_in.shape[0]
    n_seq_s, _, _, nh, hp, ns = h0.shape
    di = nh * hp
    conv_dim = w_conv.shape[-1]
    ng = (conv_dim - di) // (2 * ns)
    lc = SSM_CHUNK
    assert rows.seq_p % lc == 0 and rows.seq_s % lc == 0 and nh % ng == 0
    tm = rows.seq_s
    npt = rows.mp // tm

    tnz = _tile(di, 1024)
    tmz = _tile(rows.seq_s, 1024)
    z = _fused_matmul(h, w_in, j, [0], tm=tmz, tn=tnz, n_tiles=di // tnz, extras=[],
                      outs=[_out_tile(m, di, tmz, tnz, bf16)], epilogue=_epi_plain, name="ssd_in_z")[0]
    tnx = _tile(di, 512)
    assert di % tnx == 0 and (2 * ng * ns) % tnx == 0
    cw = w_conv.shape[1]
    bcv = b_conv.reshape(nl, 1, conv_dim)
    conv_epi = functools.partial(_conv_silu_epilogue, chunk=rows.chunk, n_prompt_tiles=npt,
                                 seq_p=rows.seq_p, seq_s=rows.seq_s)

    def conv_part(col0, width_cols, dtype, name):
        nt = width_cols // tnx
        woff = (di + col0) // tnx
        coff = col0 // tnx
        return _fused_matmul(
            h, w_in, j, [woff], tm=tm, tn=tnx, n_tiles=nt,
            extras=[(w_conv, pl.BlockSpec((None, cw, tnx), lambda jj, i: (j, 0, coff + jj))),
                    (bcv, _bias_spec(j, tnx, coff))],
            outs=[_out_tile(m, width_cols, tm, tnx, dtype)], epilogue=conv_epi, chunk=rows.chunk, name=name)[0]

    xs = conv_part(0, di, bf16, "ssd_in_x")
    bc = conv_part(di, 2 * ng * ns, bf16, "ssd_in_bc")
    ndt = 2 * nh
    dt = _fused_matmul(h, w_in, j, [(di + conv_dim) // ndt], tm=tmz, tn=ndt, n_tiles=1,
                       extras=[(dt_bias.reshape(nl, 1, ndt), _bias_spec(j, ndt))],
                       outs=[_out_tile(m, ndt, tmz, ndt, f32)], epilogue=_softplus_epilogue, name="ssd_in_dt")[0]

    tp, ts = rows.mp // lc, rows.ms // lc
    ncp, ncs = rows.seq_p // lc, rows.seq_s // lc
    nt = tp + ts
    assert lc == SLAB_LANES and ns == lc and 2 * hp == SLAB_LANES and (nh // ng) % 2 == 0 and ng % 2 == 0
    chunk = lambda dd, t: jnp.where(dd == 0, t, nt - 1 - t)
    rowmap = lambda dd, t: (chunk(dd, t), 0)
    y2, new_state = pl.pallas_call(
        functools.partial(_ssd_scan_kernel, nh=nh, hp=hp, ns=ns, ng=ng, tp=tp, ncp=ncp, ncs=ncs),
        grid=(2, nt),
        in_specs=[pl.BlockSpec((lc, di), rowmap),
                  pl.BlockSpec((lc, 2 * ng * ns), rowmap),
                  pl.BlockSpec((lc, ndt), rowmap),
                  pl.BlockSpec((None, 1, ndt), lambda dd, t: (j, 0, 0)),
                  pl.BlockSpec((None, None, None, nh, hp, ns),
                               lambda dd, t: (jnp.clip((chunk(dd, t) - tp) // ncs, 0, n_seq_s - 1), j, dd, 0, 0, 0))],
        out_specs=[pl.BlockSpec((None, lc, di), lambda dd, t: (dd, chunk(dd, t), 0)),
                   pl.BlockSpec((None, None, nh, hp, ns),
                                lambda dd, t: (jnp.clip(chunk(dd, t) // ncp, 0, n_prompt - 1), dd, 0, 0, 0))],
        out_shape=[jax.ShapeDtypeStruct((2, m, di), bf16),
                   jax.ShapeDtypeStruct((n_prompt, 2, nh, hp, ns), f32)],
        scratch_shapes=[pltpu.VMEM((ns, di), f32)] + [pltpu.VMEM((nh, lc), f32)] * 4
                       + [pltpu.VMEM((ng, lc, nh // ng), f32)],
        compiler_params=_params(2), name="ssd_scan",
    )(xs, bc, dt, a_log.reshape(nl, 1, ndt), h0)

    tg = _tile(rows.seq_s, 256)
    vrow = pl.BlockSpec((tg, di), lambda i: (i, 0))
    gated = pl.pallas_call(
        _ssd_gate_kernel, grid=(m // tg,),
        in_specs=[pl.BlockSpec((2, tg, di), lambda i: (0, i, 0)), vrow, vrow,
                  pl.BlockSpec((None, 1, di), lambda i: (j, 0, 0)),
                  pl.BlockSpec((None, 1, di), lambda i: (j, 0, 0))],
        out_specs=vrow, out_shape=jax.ShapeDtypeStruct((m, di), bf16),
        compiler_params=_params(1), name="ssd_gate",
    )(y2, xs, z, jnp.repeat(d_skip, hp, axis=1).reshape(nl, 1, di), norm_g.reshape(nl, 1, di))
    tmo, tno = _tile(m, 512), _tile(d, 512)
    out = _fused_matmul(gated, w_out, j, [0], tm=tmo, tn=tno, n_tiles=d // tno, extras=[],
                        outs=[_out_tile(m, d, tmo, tno, BRANCH_DTYPE)], epilogue=_epi_plain, name="ssd_out")[0]
    return out, new_state


def _rope(x, cos, sin):
    n = x.shape[-1]
    lane = lax.broadcasted_iota(jnp.int32, x.shape, x.ndim - 1)
    partner = jnp.where((lane & 1) == 0, pltpu.roll(x, n - 1, axis=x.ndim - 1), pltpu.roll(x, 1, axis=x.ndim - 1))
    return x * cos + partner * sin


def _q_epilogue(accs, e_refs, o_refs, i, *, hd, scale):
    gain_ref, cos_ref, sin_ref = e_refs
    acc = accs[0]
    cos, sin, gain = cos_ref[...], sin_ref[...], gain_ref[...]
    for hh in range(acc.shape[1] // hd):
        q = _rope(_rms(acc[:, hh * hd:(hh + 1) * hd], gain), cos, sin) * scale
        o_refs[0][:, hh * hd:(hh + 1) * hd] = q.astype(o_refs[0].dtype)


def _kv_epilogue(accs, e_refs, o_refs, i, *, hd, nkv):
    gain_ref, cos_ref, sin_ref = e_refs
    acc = accs[0]
    cos, sin, gain = cos_ref[...], sin_ref[...], gain_ref[...]
    o_refs[1][...] = acc
    for hh in range(nkv):
        kn = _rms(acc[:, hh * hd:(hh + 1) * hd], gain)
        o_refs[1][:, hh * hd:(hh + 1) * hd] = kn
        o_refs[0][:, hh * hd:(hh + 1) * hd] = _rope(kn, cos, sin).astype(o_refs[0].dtype)
    o_refs[0][:, nkv * hd:] = acc[:, nkv * hd:].astype(o_refs[0].dtype)


def _attn_kernel(q_ref, k_ref, v_ref, *rest, group, hd, cached):
    if cached:
        ck_ref, cv_ref, o_ref = rest
        ck = ck_ref[...].astype(bf16)
        cv = cv_ref[...].astype(bf16)
    else:
        (o_ref,) = rest
    k = k_ref[...]
    v = v_ref[...]
    nt = (((1,), (1,)), ((), ()))
    for g in range(group):
        q = q_ref[:, g * hd:(g + 1) * hd]
        s = lax.dot_general(q, k, nt, preferred_element_type=f32)
        mx = jnp.max(s, axis=-1, keepdims=True)
        if cached:
            s2 = lax.dot_general(q, ck, nt, preferred_element_type=f32)
            mx = jnp.maximum(mx, jnp.max(s2, axis=-1, keepdims=True))
        p = jnp.exp(s - mx)
        den = jnp.sum(p, axis=-1, keepdims=True)
        o = jnp.dot(p.astype(bf16), v, preferred_element_type=f32)
        if cached:
            p2 = jnp.exp(s2 - mx)
            den = den + jnp.sum(p2, axis=-1, keepdims=True)
            o = o + jnp.dot(p2.astype(bf16), cv, preferred_element_type=f32)
        o_ref[:, g * hd:(g + 1) * hd] = (o / den).astype(o_ref.dtype)


def _attention(h, j, cache_k, cache_v, w_qkv, q_norm, k_norm, w_o, rows, n_prompt):
    m, d = h.shape
    nl = w_qkv.shape[0]
    hd = q_norm.shape[-1]
    n_seq_s, _, past, nkv, _ = cache_k.shape
    nq = w_o.shape[1] // hd
    group = nq // nkv
    assert rows.seq_s % GRID_W == 0
    pos = jnp.arange(rows.seq_s)
    inv = ROPE_THETA ** (-jnp.arange(hd // 4, dtype=f32) / (hd // 4))
    ang = jnp.concatenate([(pos // GRID_W).astype(f32)[:, None] * inv, (pos % GRID_W).astype(f32)[:, None] * inv], -1)
    cos_s = jnp.repeat(jnp.cos(ang), 2, axis=-1)
    sin_s = jnp.stack([-jnp.sin(ang), jnp.sin(ang)], axis=-1).reshape(rows.seq_s, hd)
    reps = rows.ms // rows.seq_s
    cos_t = jnp.concatenate([jnp.ones((rows.mp, hd), f32), jnp.tile(cos_s, (reps, 1))], 0)
    sin_t = jnp.concatenate([jnp.zeros((rows.mp, hd), f32), jnp.tile(sin_s, (reps, 1))], 0)

    tm = _tile(rows.seq_s, 1024)
    tab = pl.BlockSpec((tm, hd), lambda jj, i: (i, 0))
    gspec = pl.BlockSpec((None, 1, hd), lambda jj, i: (j, 0, 0))
    nqc, nkc = nq * hd, nkv * hd
    tnq = _tile(nqc, 1024)
    q = _fused_matmul(h, w_qkv, j, [0], tm=tm, tn=tnq, n_tiles=nqc // tnq,
                      extras=[(q_norm.reshape(nl, 1, hd), gspec), (cos_t, tab), (sin_t, tab)],
                      outs=[_out_tile(m, nqc, tm, tnq, bf16)],
                      epilogue=functools.partial(_q_epilogue, hd=hd, scale=hd ** -0.5), name="attn_q")[0]
    tnk = 2 * nkc
    assert nqc % tnk == 0
    kv, kv_f32 = _fused_matmul(h, w_qkv, j, [nqc // tnk], tm=tm, tn=tnk, n_tiles=1,
                               extras=[(k_norm.reshape(nl, 1, hd), gspec), (cos_t, tab), (sin_t, tab)],
                               outs=[_out_tile(m, tnk, tm, tnk, bf16), _out_tile(m, tnk, tm, tnk, f32)],
                               epilogue=functools.partial(_kv_epilogue, hd=hd, nkv=nkv), name="attn_kv")

    gw = group * hd
    sp = rows.seq_p
    o_p = pl.pallas_call(
        functools.partial(_attn_kernel, group=group, hd=hd, cached=False),
        grid=(n_prompt, nkv),
        in_specs=[pl.BlockSpec((sp, gw), lambda b, kh: (b, kh)),
                  pl.BlockSpec((sp, hd), lambda b, kh: (b, kh)),
                  pl.BlockSpec((sp, hd), lambda b, kh: (b, nkv + kh))],
        out_specs=pl.BlockSpec((sp, gw), lambda b, kh: (b, kh)),
        out_shape=jax.ShapeDtypeStruct((rows.mp, nqc), bf16),
        compiler_params=_params(2), name="attn_prompt",
    )(q, kv, kv)
    ss = rows.seq_s
    tq = _tile(ss, 512)
    nqb = ss // tq
    pb = rows.mp // ss
    pbq = rows.mp // tq
    ck = cache_k[:, j].reshape(n_seq_s, past, nkc)
    cv = cache_v[:, j].reshape(n_seq_s, past, nkc)
    o_s = pl.pallas_call(
        functools.partial(_attn_kernel, group=group, hd=hd, cached=True),
        grid=(n_seq_s, nkv, nqb),
        in_specs=[pl.BlockSpec((tq, gw), lambda b, kh, qi: (pbq + b * nqb + qi, kh)),
                  pl.BlockSpec((ss, hd), lambda b, kh, qi: (pb + b, kh)),
                  pl.BlockSpec((ss, hd), lambda b, kh, qi: (pb + b, nkv + kh)),
                  pl.BlockSpec((None, past, hd), lambda b, kh, qi: (b, 0, kh)),
                  pl.BlockSpec((None, past, hd), lambda b, kh, qi: (b, 0, kh))],
        out_specs=pl.BlockSpec((tq, gw), lambda b, kh, qi: (b * nqb + qi, kh)),
        out_shape=jax.ShapeDtypeStruct((rows.ms, nqc), bf16),
        compiler_params=_params(3), name="attn_sample",
    )(q, kv, kv, ck, cv)
    o = jnp.concatenate([o_p, o_s], axis=0)
    tmo, tno = _tile(m, 1024), _tile(d, 1024)
    out = _fused_matmul(o, w_o, j, [0], tm=tmo, tn=tno, n_tiles=d // tno, extras=[],
                        outs=[_out_tile(m, d, tmo, tno, BRANCH_DTYPE)], epilogue=_epi_plain, name="attn_o")[0]
    new_k = kv_f32[:rows.mp, :nkc].reshape(n_prompt, sp, nkv, hd)
    new_v = kv_f32[:rows.mp, nkc:].reshape(n_prompt, sp, nkv, hd)
    return out, new_k, new_v


def kernel(x_prompt, x_sample, c, state_ssd, cache_k, cache_v, c_ctx, w_mod, b_mod, norm_pre, norm_post, cv_w_pw1, cv_b_pw1, cv_w_dw, cv_b_dw, cv_ln_g, cv_ln_b, cv_w_pw2, cv_b_pw2, ssd_w_in, ssd_w_conv, ssd_b_conv, ssd_dt_bias, ssd_a_log, ssd_d, ssd_norm_g, ssd_w_out, attn_w_qkv, attn_q_norm, attn_k_norm, attn_w_o, ffn_w_up, ffn_w_dw, ffn_b_dw, ffn_w_down):
    nb, sp, d = x_prompt.shape
    ndb, ss, _ = x_sample.shape
    depth = w_mod.shape[0]
    rows = _Rows(nb * sp, ndb * ss, sp, ss)
    assert sp & (sp - 1) == 0 and ss & (ss - 1) == 0, "sequence lengths must be powers of two"
    assert rows.mp % ss == 0 and 1 + ndb <= COND_ROWS

    y = (x_prompt.reshape(rows.mp, d), x_sample.reshape(rows.ms, d))
    cond = jnp.concatenate([c_ctx[None, :], c, jnp.zeros((COND_ROWS - 1 - ndb, d), f32)], axis=0)
    mod = _modulation(cond, w_mod, b_mod).reshape(depth * COND_ROWS, 6, d)
    gpre = norm_pre.reshape(depth * 2, 1, d)
    gpost = norm_post.reshape(depth * 2, 1, d)
    h = _prenorm(y, gpre, 0, mod, 0, 1, 0, rows)
    new_ssd, new_k, new_v = [], [], []
    for i in range(depth):
        kind, j = i % N_MIXERS, i // N_MIXERS
        if kind == 0:
            o = _conformer(h, j, cv_w_pw1, cv_b_pw1, cv_w_dw, cv_b_dw, cv_ln_g, cv_ln_b, cv_w_pw2, cv_b_pw2, rows)
        elif kind == 1:
            o, st = _ssd_mixer(h, j, state_ssd, ssd_w_in, ssd_w_conv, ssd_b_conv, ssd_dt_bias, ssd_a_log,
                               ssd_d, ssd_norm_g, ssd_w_out, rows, nb)
            new_ssd.append(st)
        else:
            o, k_new, v_new = _attention(h, j, cache_k, cache_v, attn_w_qkv, attn_q_norm, attn_k_norm,
                                         attn_w_o, rows, nb)
            new_k.append(k_new)
            new_v.append(v_new)
        y, h = _resid(y, o, gpost, 2 * i, mod, i, 2, rows, nxt=(gpre, 2 * i + 1, i, 4, 3))
        o = _conv_ffn(h, i, ffn_w_up, ffn_w_dw, ffn_b_dw, ffn_w_down, rows)
        if i + 1 < depth:
            y, h = _resid(y, o, gpost, 2 * i + 1, mod, i, 5, rows, nxt=(gpre, 2 * i + 2, i + 1, 1, 0))
        else:
            yp, ys = _resid(y, o, gpost, 2 * i + 1, mod, i, 5, rows, split_out=True)
    yp = yp.reshape(nb, sp, d)
    ys = ys.reshape(ndb, ss, d)
    return (yp, ys, jnp.stack(new_ssd, axis=1), jnp.stack(new_k, axis=1), jnp.stack(new_v, axis=1))
```

```python
import functools
import math

import jax
import jax.numpy as jnp
from jax import lax
from jax.experimental import pallas as pl
from jax.experimental.pallas import tpu as pltpu

f32 = jnp.float32
bf16 = jnp.bfloat16
BRANCH_DTYPE = bf16

EPS = 1e-6
GRID_W = 64
ROPE_THETA = 10000.0
SSM_CHUNK = 128
N_MIXERS = 3
V7X_VMEM_LIMIT_BYTES = 56 * 1024 * 1024
COND_ROWS = 8
ROW_HALO = 8
ROW_CHUNK = 256
CONV_BLOCK = 256
SLAB_LANES = 128
EPILOGUE_LAG = 2


def _params(n_axes, vmem=V7X_VMEM_LIMIT_BYTES):
    return pltpu.CompilerParams(dimension_semantics=("arbitrary",) * n_axes,
                                vmem_limit_bytes=vmem)


def _tile(dim, pref):
    return pref if dim % pref == 0 else dim


def _sigmoid(x):
    return 1.0 / (1.0 + jnp.exp(-x))


def _silu(x):
    return x * _sigmoid(x)


def _rms(x, g):
    return x * lax.rsqrt(jnp.mean(x * x, axis=-1, keepdims=True) + EPS) * g


def _mod_kernel(c_ref, w_ref, b_ref, o_ref):
    s = _silu(c_ref[...]).astype(bf16)
    o_ref[...] = jnp.dot(s, w_ref[...].astype(bf16), preferred_element_type=f32) + b_ref[...]


def _modulation(cond, w_mod, b_mod):
    depth, d, n = w_mod.shape
    tn = _tile(n, 1024)
    return pl.pallas_call(
        _mod_kernel,
        grid=(depth, n // tn),
        in_specs=[pl.BlockSpec((COND_ROWS, d), lambda l, j: (0, 0)),
                  pl.BlockSpec((None, d, tn), lambda l, j: (l, 0, j)),
                  pl.BlockSpec((None, 1, tn), lambda l, j: (l, 0, j))],
        out_specs=pl.BlockSpec((None, COND_ROWS, tn), lambda l, j: (l, 0, j)),
        out_shape=jax.ShapeDtypeStruct((depth, COND_ROWS, n), f32),
        compiler_params=_params(2), name="modulation",
    )(cond, w_mod, b_mod.reshape(depth, 1, n))


def _prenorm_kernel(yp_ref, ys_ref, g_ref, mod_ref, h_ref, *, sc, sh, n_prompt_tiles):
    y = jnp.where(pl.program_id(0) < n_prompt_tiles, yp_ref[...], ys_ref[...])
    yn = _rms(y, g_ref[...])
    h_ref[...] = (yn * (1.0 + mod_ref[sc:sc + 1, :]) + mod_ref[sh:sh + 1, :]).astype(h_ref.dtype)


def _resid_kernel(*refs, ga, nxt, split_in, split_out, n_prompt_tiles):
    i = pl.program_id(0)
    if split_in:
        yp_ref, ys_ref, o_ref, gpost_ref, moda_ref, *rest = refs
        y_in = jnp.where(i < n_prompt_tiles, yp_ref[...], ys_ref[...])
    else:
        y_ref, o_ref, gpost_ref, moda_ref, *rest = refs
        y_in = y_ref[...]
    y = y_in + moda_ref[ga:ga + 1, :] * _rms(o_ref[...].astype(f32), gpost_ref[...])
    if split_out:
        outp_ref, outs_ref = rest

        @pl.when(i < n_prompt_tiles)
        def _():
            outp_ref[...] = y

        @pl.when(i >= n_prompt_tiles)
        def _():
            outs_ref[...] = y
    elif nxt is None:
        (ynew_ref,) = rest
        ynew_ref[...] = y
    else:
        gpre_ref, modb_ref, ynew_ref, h_ref = rest
        sc, sh = nxt
        ynew_ref[...] = y
        yn = _rms(y, gpre_ref[...])
        h_ref[...] = (yn * (1.0 + modb_ref[sc:sc + 1, :]) + modb_ref[sh:sh + 1, :]).astype(h_ref.dtype)


class _Rows:
    def __init__(self, mp, ms, seq_p, seq_s):
        self.mp, self.ms, self.m = mp, ms, mp + ms
        self.seq_p, self.seq_s = seq_p, seq_s
        self.chunk = _tile(seq_p, ROW_CHUNK)
        assert seq_p % self.chunk == 0 and seq_s % self.chunk == 0

    def cond_of_tile(self, tm):
        assert self.mp % tm == 0 and self.seq_s % tm == 0
        npt, per = self.mp // tm, self.seq_s // tm
        return lambda i: jnp.where(i < npt, 0, 1 + (i - npt) // per)


def _split_specs(rows, tm, d):
    npt = rows.mp // tm
    return [pl.BlockSpec((tm, d), lambda i: (jnp.minimum(i, npt - 1), 0)),
            pl.BlockSpec((tm, d), lambda i: (jnp.maximum(i - npt, 0), 0))]


def _prenorm(y_pair, g_all, g_idx, mod, layer, sc, sh, rows):
    d = y_pair[0].shape[1]
    m = rows.m
    tm = _tile(rows.seq_s, 512)
    cond = rows.cond_of_tile(tm)
    return pl.pallas_call(
        functools.partial(_prenorm_kernel, sc=sc, sh=sh, n_prompt_tiles=rows.mp // tm),
        grid=(m // tm,),
        in_specs=_split_specs(rows, tm, d)
                 + [pl.BlockSpec((None, 1, d), lambda i: (g_idx, 0, 0)),
                    pl.BlockSpec((None, 6, d), lambda i: (layer * COND_ROWS + cond(i), 0, 0))],
        out_specs=pl.BlockSpec((tm, d), lambda i: (i, 0)),
        out_shape=jax.ShapeDtypeStruct((m, d), bf16),
        compiler_params=_params(1), name="prenorm",
    )(*y_pair, g_all, mod)


def _resid(y, o, gpost_all, gpost_idx, mod, layer, ga, rows, nxt=None, split_out=False):
    m, d = o.shape
    tm = _tile(rows.seq_s, 512)
    cond = rows.cond_of_tile(tm)
    row_spec = pl.BlockSpec((tm, d), lambda i: (i, 0))
    split_in = isinstance(y, tuple)
    in_specs = (_split_specs(rows, tm, d) if split_in else [row_spec]) + [
        row_spec,
        pl.BlockSpec((None, 1, d), lambda i: (gpost_idx, 0, 0)),
        pl.BlockSpec((None, 6, d), lambda i: (layer * COND_ROWS + cond(i), 0, 0))]
    args = (list(y) if split_in else [y]) + [o, gpost_all, mod]
    if split_out:
        assert nxt is None
        out_specs = _split_specs(rows, tm, d)
        out_shape = [jax.ShapeDtypeStruct((rows.mp, d), f32), jax.ShapeDtypeStruct((rows.ms, d), f32)]
        kn = None
    elif nxt is None:
        out_specs, out_shape, kn = row_spec, jax.ShapeDtypeStruct((m, d), f32), None
    else:
        gpre_all, gpre_idx, nlayer, sc, sh = nxt
        in_specs += [pl.BlockSpec((None, 1, d), lambda i: (gpre_idx, 0, 0)),
                     pl.BlockSpec((None, 6, d), lambda i: (nlayer * COND_ROWS + cond(i), 0, 0))]
        args += [gpre_all, mod]
        out_specs = [row_spec, row_spec]
        out_shape = [jax.ShapeDtypeStruct((m, d), f32), jax.ShapeDtypeStruct((m, d), bf16)]
        kn = (sc, sh)
    return pl.pallas_call(
        functools.partial(_resid_kernel, ga=ga, nxt=kn, split_in=split_in, split_out=split_out,
                          n_prompt_tiles=rows.mp // tm),
        grid=(m // tm,), in_specs=in_specs, out_specs=out_specs, out_shape=out_shape,
        compiler_params=_params(1), name="resid",
    )(*args)


def _fmm_kernel(*refs, n_slab, n_extra, n_out, epilogue, chunk):
    x_ref = refs[0]
    w_refs = refs[1:1 + n_slab]
    e_refs = refs[1 + n_slab:1 + n_slab + n_extra]
    o_refs = refs[1 + n_slab + n_extra:1 + n_slab + n_extra + n_out]
    wb_refs = refs[1 + n_slab + n_extra + n_out:1 + 2 * n_slab + n_extra + n_out]
    u_refs = refs[1 + 2 * n_slab + n_extra + n_out:]
    i = pl.program_id(1)

    @pl.when(i == 0)
    def _():
        for w_ref, wb_ref in zip(w_refs, wb_refs):
            wb_ref[...] = w_ref[...].astype(bf16)

    if chunk is None:
        x = x_ref[...]
        accs = [jnp.dot(x, wb_ref[...], preferred_element_type=f32) for wb_ref in wb_refs]
        if u_refs:
            for u_ref, acc in zip(u_refs, accs):
                u_ref[...] = acc
            accs = u_refs
        epilogue(accs, e_refs, o_refs, i)
        return

    tm = x_ref.shape[0]
    n_chunk = tm // chunk
    per_slab = [u_refs[s * n_chunk:(s + 1) * n_chunk] for s in range(n_slab)]
    zeros = jnp.zeros((ROW_HALO, u_refs[0].shape[1]), f32)
    for bufs in per_slab:
        bufs[0][0:ROW_HALO, :] = zeros
        bufs[-1][ROW_HALO + chunk:, :] = zeros
    for c in range(n_chunk + EPILOGUE_LAG):
        if c < n_chunk:
            xc = x_ref[c * chunk:(c + 1) * chunk, :]
            for wb_ref, bufs in zip(wb_refs, per_slab):
                prod = jnp.dot(xc, wb_ref[...], preferred_element_type=f32)
                bufs[c][ROW_HALO:ROW_HALO + chunk, :] = prod
                if c > 0:
                    bufs[c - 1][ROW_HALO + chunk:, :] = prod[0:ROW_HALO]
                if c + 1 < n_chunk:
                    bufs[c + 1][0:ROW_HALO, :] = prod[chunk - ROW_HALO:]
        if c >= EPILOGUE_LAG:
            done = c - EPILOGUE_LAG
            epilogue(done, [bufs[done] for bufs in per_slab], e_refs, o_refs, i)


def _fused_matmul(x, w_all, layer, slab_offs, *, tm, tn, n_tiles, extras, outs, epilogue, name, chunk=None,
                  buffered=False):
    m, k = x.shape
    in_specs = [pl.BlockSpec((tm, k), lambda j, i: (i, 0))]
    args = [x]
    for off in slab_offs:
        in_specs.append(pl.BlockSpec((None, k, tn), lambda j, i, off=off: (layer, 0, off + j)))
        args.append(w_all)
    for arr, spec in extras:
        in_specs.append(spec)
        args.append(arr)
    scratch = [pltpu.VMEM((k, tn), bf16) for _ in slab_offs]
    if chunk is not None:
        assert tm % chunk == 0 and not buffered
        scratch += [pltpu.VMEM((chunk + 2 * ROW_HALO, tn), f32)] * (len(slab_offs) * (tm // chunk))
    if buffered:
        scratch += [pltpu.VMEM((tm, tn), f32) for _ in slab_offs]
    return pl.pallas_call(
        functools.partial(_fmm_kernel, n_slab=len(slab_offs), n_extra=len(extras),
                          n_out=len(outs), epilogue=epilogue, chunk=chunk),
        grid=(n_tiles, m // tm),
        in_specs=in_specs,
        out_specs=[spec for _, spec in outs],
        out_shape=[sds for sds, _ in outs],
        scratch_shapes=scratch,
        compiler_params=_params(2), name=name,
    )(*args)


def _bias_spec(layer, tn, off=0):
    return pl.BlockSpec((None, 1, tn), lambda j, i: (layer, 0, off + j))


def _out_tile(m, n, tm, tn, dtype):
    return (jax.ShapeDtypeStruct((m, n), dtype), pl.BlockSpec((tm, tn), lambda j, i: (i, j)))


def _epi_plain(accs, e_refs, o_refs, i):
    o_refs[0][...] = accs[0].astype(o_refs[0].dtype)


def _epi_bias(accs, e_refs, o_refs, i):
    o_refs[0][...] = (accs[0] + e_refs[0][...]).astype(o_refs[0].dtype)


class _ChunkConv:
    def __init__(self, u_ref, w, bias, c, chunk, seq):
        self.u_ref, self.chunk = u_ref, chunk
        self.width = w.shape[0]
        self.left = (self.width - 1) // 2
        assert self.left <= ROW_HALO and self.width - 1 - self.left <= ROW_HALO
        assert chunk % CONV_BLOCK == 0 and CONV_BLOCK >= 2 * ROW_HALO
        tn = u_ref.shape[1]
        self.starts_seq = ((c * chunk) & (seq - 1)) == 0
        self.ends_seq = (((c + 1) * chunk) & (seq - 1)) == 0
        self.w_tile = [jnp.broadcast_to(w[k:k + 1, :], (ROW_HALO, tn)) for k in range(self.width)]
        self.w_block = [jnp.tile(wt, (CONV_BLOCK // ROW_HALO, 1)) for wt in self.w_tile]
        self.b_tile = jnp.broadcast_to(bias, (ROW_HALO, tn))
        self.b_block = jnp.tile(self.b_tile, (CONV_BLOCK // ROW_HALO, 1))

    def _taps(self, r0, n_rows, masked):
        w_rows = self.w_tile if n_rows == ROW_HALO else self.w_block
        acc = self.b_tile if n_rows == ROW_HALO else self.b_block
        far = self.chunk + ROW_HALO
        if masked:
            row = lax.broadcasted_iota(jnp.int32, acc.shape, 0) + r0
        for k in range(self.width):
            d = k - self.left
            v = self.u_ref[pl.ds(ROW_HALO + r0 + d, n_rows), :]
            if masked and d < 0:
                v = jnp.where(row >= jnp.where(self.starts_seq, -d, -far), v, 0.0)
            if masked and d > 0:
                v = jnp.where(row < jnp.where(self.ends_seq, self.chunk - d, far), v, 0.0)
            acc = acc + v * w_rows[k]
        return acc

    def block(self, r0):
        out = self._taps(r0, CONV_BLOCK, False)
        if r0 == 0:
            out = jnp.concatenate([self._taps(0, ROW_HALO, True), out[ROW_HALO:]], axis=0)
        if r0 + CONV_BLOCK == self.chunk:
            tail = self._taps(self.chunk - ROW_HALO, ROW_HALO, True)
            out = jnp.concatenate([out[:CONV_BLOCK - ROW_HALO], tail], axis=0)
        return out


def _ffn_up_epilogue(c, u_refs, e_refs, o_refs, i, *, chunk, n_prompt_tiles, seq_p, seq_s):
    wa_ref, wg_ref, ba_ref, bg_ref = e_refs
    seq = jnp.where(i < n_prompt_tiles, seq_p, seq_s)
    conv_a = _ChunkConv(u_refs[0], wa_ref[...], ba_ref[...], c, chunk, seq)
    conv_g = _ChunkConv(u_refs[1], wg_ref[...], bg_ref[...], c, chunk, seq)
    for r0 in range(0, chunk, CONV_BLOCK):
        a, g = conv_a.block(r0), conv_g.block(r0)
        o_refs[0][c * chunk + r0:c * chunk + r0 + CONV_BLOCK, :] = (_silu(g) * a).astype(o_refs[0].dtype)


def _conv_ffn(h, layer, w_up, w_dw, b_dw, w_down, rows):
    m, d = h.shape
    dff = w_down.shape[1]
    depth = w_up.shape[0]
    tm = rows.seq_s
    tn = _tile(dff, 512)
    nt = dff // tn
    width = w_dw.shape[1]
    b3 = b_dw.reshape(depth, 1, 2 * dff)
    dw_spec = lambda off: pl.BlockSpec((None, width, tn), lambda j, i: (layer, 0, off + j))
    act = _fused_matmul(
        h, w_up, layer, [0, nt], tm=tm, tn=tn, n_tiles=nt,
        extras=[(w_dw, dw_spec(0)), (w_dw, dw_spec(nt)), (b3, _bias_spec(layer, tn)), (b3, _bias_spec(layer, tn, nt))],
        outs=[_out_tile(m, dff, tm, tn, bf16)], chunk=rows.chunk,
        epilogue=functools.partial(_ffn_up_epilogue, chunk=rows.chunk, n_prompt_tiles=rows.mp // tm,
                                   seq_p=rows.seq_p, seq_s=rows.seq_s),
        name="ffn_up")[0]
    tm2, tn2 = _tile(m, 512), _tile(d, 512)
    return _fused_matmul(act, w_down, layer, [0], tm=tm2, tn=tn2, n_tiles=d // tn2, extras=[],
                         outs=[_out_tile(m, d, tm2, tn2, BRANCH_DTYPE)], epilogue=_epi_plain, name="ffn_down")[0]


def _glu_epilogue(accs, e_refs, o_refs, i):
    a = accs[0] + e_refs[0][...]
    g = accs[1] + e_refs[1][...]
    o_refs[0][...] = (a * _sigmoid(g)).astype(o_refs[0].dtype)


CONV_HALO = 16
CONV_LANES = 128
CONV_ROWS = 128
LN_ROWS = 32
QK_ROWS = 128


def _cv_mid_kernel(u_ref, prev_ref, next_ref, w_ref, b_ref, g_ref, beta_ref, o_ref, pad_ref, conv_ref,
                   *, blocks_p, per_seq_p, per_seq_s):
    i = pl.program_id(0)
    rb, d = u_ref.shape
    width = w_ref.shape[0]
    left = (width - 1) // 2
    per = jnp.where(i < blocks_p, per_seq_p, per_seq_s)
    k_in_seq = jnp.where(i < blocks_p, i, i - blocks_p) % per
    first = k_in_seq == 0
    last = k_in_seq == per - 1
    pad_ref[0:CONV_HALO, :] = jnp.where(first, 0.0, prev_ref[...])
    pad_ref[CONV_HALO:CONV_HALO + rb, :] = u_ref[...]
    pad_ref[CONV_HALO + rb:CONV_HALO + rb + CONV_HALO, :] = jnp.where(last, 0.0, next_ref[...])

    def lane_chunk(cidx, carry):
        lo = pl.multiple_of(cidx * CONV_LANES, CONV_LANES)
        wc = w_ref[:, pl.ds(lo, CONV_LANES)]
        for r0 in range(0, rb, CONV_ROWS):
            partial = []
            for phase in range(ROW_HALO):
                ks = [k for k in range(width) if (CONV_HALO - left + k) % ROW_HALO == phase]
                if not ks:
                    continue
                span = max(CONV_HALO - left + k for k in ks) - phase
                shifted = pad_ref[pl.ds(r0 + phase, CONV_ROWS + span), pl.ds(lo, CONV_LANES)]
                acc = None
                for k in ks:
                    off = CONV_HALO - left + k - phase
                    term = shifted[off:off + CONV_ROWS] * wc[k:k + 1, :]
                    acc = term if acc is None else acc + term
                partial.append(acc)
            total = partial[0]
            for acc in partial[1:]:
                total = total + acc
            conv_ref[pl.ds(r0, CONV_ROWS), pl.ds(lo, CONV_LANES)] = total
        return carry

    lax.fori_loop(0, d // CONV_LANES, lane_chunk, 0)

    def norm_rows(ridx, carry):
        rs = pl.ds(pl.multiple_of(ridx * LN_ROWS, LN_ROWS), LN_ROWS)
        u = conv_ref[rs, :] + b_ref[...]
        mu = jnp.mean(u, axis=-1, keepdims=True)
        uc = u - mu
        v = uc * lax.rsqrt(jnp.mean(uc * uc, axis=-1, keepdims=True) + EPS) * g_ref[...] + beta_ref[...]
        o_ref[rs, :] = _silu(v).astype(o_ref.dtype)
        return carry

    lax.fori_loop(0, rb // LN_ROWS, norm_rows, 0, unroll=2)


def _conformer(h, j, w_pw1, b_pw1, w_dw, b_dw, ln_g, ln_b, w_pw2, b_pw2, rows):
    m, d = h.shape
    nl = w_pw1.shape[0]
    tm, tn = _tile(rows.seq_s, 1024), _tile(d, 512)
    nt = d // tn
    b1 = b_pw1.reshape(nl, 1, 2 * d)
    glu = _fused_matmul(h, w_pw1, j, [0, nt], tm=tm, tn=tn, n_tiles=nt,
                        extras=[(b1, _bias_spec(j, tn)), (b1, _bias_spec(j, tn, nt))],
                        outs=[_out_tile(m, d, tm, tn, f32)], epilogue=_glu_epilogue, name="cv_pw1")[0]
    width = w_dw.shape[1]
    assert (width - 1) // 2 <= CONV_HALO
    rb = _tile(rows.seq_p, 256)
    assert rows.seq_p % rb == 0 and rows.seq_s % rb == 0 and rb % CONV_ROWS == 0 and d % CONV_LANES == 0
    hb = rb // CONV_HALO
    n_halo = m // CONV_HALO
    vec = lambda a: a.reshape(nl, 1, d)
    vspec = pl.BlockSpec((None, 1, d), lambda i: (j, 0, 0))
    mid = pl.pallas_call(
        functools.partial(_cv_mid_kernel, blocks_p=rows.mp // rb, per_seq_p=rows.seq_p // rb,
                          per_seq_s=rows.seq_s // rb),
        grid=(m // rb,),
        in_specs=[pl.BlockSpec((rb, d), lambda i: (i, 0)),
                  pl.BlockSpec((CONV_HALO, d), lambda i: (jnp.maximum(i * hb - 1, 0), 0)),
                  pl.BlockSpec((CONV_HALO, d), lambda i: (jnp.minimum((i + 1) * hb, n_halo - 1), 0)),
                  pl.BlockSpec((None, width, d), lambda i: (j, 0, 0)),
                  vspec, vspec, vspec],
        out_specs=pl.BlockSpec((rb, d), lambda i: (i, 0)),
        out_shape=jax.ShapeDtypeStruct((m, d), bf16),
        scratch_shapes=[pltpu.VMEM((rb + 2 * CONV_HALO, d), f32), pltpu.VMEM((rb, d), f32)],
        compiler_params=_params(1), name="cv_mid",
    )(glu, glu, glu, w_dw, vec(b_dw), vec(ln_g), vec(ln_b))
    tm2, tn2 = _tile(m, 1024), _tile(d, 1024)
    return _fused_matmul(mid, w_pw2, j, [0], tm=tm2, tn=tn2, n_tiles=d // tn2,
                         extras=[(vec(b_pw2), _bias_spec(j, tn2))],
                         outs=[_out_tile(m, d, tm2, tn2, BRANCH_DTYPE)], epilogue=_epi_bias, name="cv_pw2")[0]


def _conv_silu_epilogue(c, u_refs, e_refs, o_refs, i, *, chunk, n_prompt_tiles, seq_p, seq_s):
    w_ref, b_ref = e_refs
    seq = jnp.where(i < n_prompt_tiles, seq_p, seq_s)
    conv = _ChunkConv(u_refs[0], w_ref[...], b_ref[...], c, chunk, seq)
    for r0 in range(0, chunk, CONV_BLOCK):
        o_refs[0][c * chunk + r0:c * chunk + r0 + CONV_BLOCK, :] = _silu(conv.block(r0)).astype(o_refs[0].dtype)


def _softplus_epilogue(accs, e_refs, o_refs, i):
    x = accs[0] + e_refs[0][...]
    o_refs[0][...] = jnp.maximum(x, 0.0) + jnp.log1p(jnp.exp(-jnp.abs(x)))


def _ssd_scan_kernel(xs_ref, bc_ref, dt_ref, alog_ref, h0_ref, y_ref, st_ref,
                     state_ref, acst_ref, dtt_ref, wt_ref, cdt_ref, acsc_ref,
                     *, nh, hp, ns, ng, tp, ncp, ncs):
    d = pl.program_id(0)
    t = pl.program_id(1)
    nt = pl.num_programs(1)
    lc = xs_ref.shape[0]
    r = nh // ng
    gp = r * hp
    fwd = d == 0
    te = jnp.where(fwd, t, nt - 1 - t)
    in_prompt = te < tp
    c_in_seq = jnp.where(in_prompt, te % ncp, (te - tp) % ncs)
    n_in_seq = jnp.where(in_prompt, ncp, ncs)
    is_first = c_in_seq == jnp.where(fwd, 0, n_in_seq - 1)
    is_last = c_in_seq == jnp.where(fwd, n_in_seq - 1, 0)

    @pl.when(jnp.logical_and(is_first, in_prompt))
    def _():
        state_ref[...] = jnp.zeros_like(state_ref)

    @pl.when(jnp.logical_and(is_first, jnp.logical_not(in_prompt)))
    def _():
        for g in range(ng):
            state_ref[:, g * gp:(g + 1) * gp] = h0_ref[g * r:(g + 1) * r].reshape(gp, ns).T

    dt2 = dt_ref[...]
    da2 = dt2 * (-jnp.exp(alog_ref[...]))
    row = lax.broadcasted_iota(jnp.int32, (lc, lc), 0)
    col = lax.broadcasted_iota(jnp.int32, (lc, lc), 1)
    diff = jnp.where(fwd, row - col, col - row)
    mask = diff >= 0
    tri = jnp.where(mask, 1.0, 0.0).astype(bf16)
    hi = da2.astype(bf16)
    rest = da2 - hi.astype(f32)
    mid = rest.astype(bf16)
    low = (rest - mid.astype(f32)).astype(bf16)
    acs2 = (jnp.dot(tri, hi, preferred_element_type=f32) + jnp.dot(tri, mid, preferred_element_type=f32)
            + jnp.dot(tri, low, preferred_element_type=f32))
    acst2 = acs2.T
    dtt2 = dt2.T
    acs = jnp.where(fwd, acs2[:, :nh], acs2[:, nh:])
    acst = jnp.where(fwd, acst2[:nh], acst2[nh:])
    dtt = jnp.where(fwd, dtt2[:nh], dtt2[nh:])
    total = jnp.where(fwd, acst[:, lc - 1:lc], acst[:, 0:1])
    totb = jnp.broadcast_to(total, (nh, lc))
    acst_ref[...] = acst
    dtt_ref[...] = dtt
    wt_ref[...] = dtt * jnp.exp(totb - acst)
    cdt_ref[...] = jnp.exp(totb)
    for g in range(ng):
        acsc_ref[g] = acs[:, g * r:(g + 1) * r]
    pair = SLAB_LANES // hp
    first_lanes = lax.broadcasted_iota(jnp.int32, (1, SLAB_LANES), 1) < hp

    def group(g, carry):
        lo = pl.multiple_of(g * gp, gp)
        bg = bc_ref[:, pl.ds(pl.multiple_of(g * ns, ns), ns)]
        cg = bc_ref[:, pl.ds(pl.multiple_of((ng + g) * ns, ns), ns)]
        scores = lax.dot_general(cg, bg, (((1,), (1,)), ((), ())), preferred_element_type=f32)
        bt = bg.astype(f32).T
        cf = cg.astype(f32)
        cols = acsc_ref[g]
        hrow = pl.ds(pl.multiple_of(g * r, r), r)
        rows_a, rows_dt, rows_w, rows_cd = acst_ref[hrow, :], dtt_ref[hrow, :], wt_ref[hrow, :], cdt_ref[hrow, :]
        glanes = pl.ds(lo, gp)
        x_g = xs_ref[:, glanes].astype(bf16)
        st_g = state_ref[:, glanes]
        ys, sts = [], []
        for k in range(r // pair):
            st = st_g[:, k * SLAB_LANES:(k + 1) * SLAB_LANES]
            x_k = x_g[:, k * SLAB_LANES:(k + 1) * SLAB_LANES]
            tops, bws = [], []
            for q in (pair * k, pair * k + 1):
                a_l = jnp.broadcast_to(cols[:, q:q + 1], (lc, lc))
                seg = a_l - rows_a[q:q + 1, :]
                mh = scores * jnp.where(mask, jnp.exp(seg), 0.0) * rows_dt[q:q + 1, :]
                ch = cf * jnp.exp(a_l)
                tops.append(jnp.concatenate([mh.astype(bf16), ch.astype(bf16)], axis=1))
                bws.append((bt * rows_w[q:q + 1, :]).astype(bf16))
            out = jnp.dot(jnp.concatenate(tops, axis=0), jnp.concatenate([x_k, st.astype(bf16)], axis=0),
                          preferred_element_type=f32)
            y = jnp.where(first_lanes, out[:lc], out[lc:])
            x_0 = jnp.where(first_lanes, x_k, jnp.zeros_like(x_k))
            x_1 = jnp.where(first_lanes, jnp.zeros_like(x_k), x_k)
            cs = jnp.dot(jnp.concatenate(bws, axis=1), jnp.concatenate([x_0, x_1], axis=0),
                         preferred_element_type=f32)
            cd = jnp.where(first_lanes, rows_cd[pair * k:pair * k + 1, :], rows_cd[pair * k + 1:pair * k + 2, :])
            sts.append(st * cd + cs)
            ys.append(y.astype(y_ref.dtype))
        state_ref[:, glanes] = jnp.concatenate(sts, axis=1)
        y_ref[:, glanes] = jnp.concatenate(ys, axis=1)
        return carry

    lax.fori_loop(0, ng, group, 0, unroll=4)

    @pl.when(jnp.logical_and(is_last, in_prompt))
    def _():
        for g in range(ng):
            st_ref[g * r:(g + 1) * r] = state_ref[:, g * gp:(g + 1) * gp].T.reshape(r, hp, ns)


def _ssd_gate_kernel(y_ref, xs_ref, z_ref, dx_ref, g_ref, o_ref):
    y = y_ref[0].astype(f32) + y_ref[1].astype(f32) + dx_ref[...] * xs_ref[...].astype(f32)
    y = y * _silu(z_ref[...].astype(f32))
    o_ref[...] = _rms(y, g_ref[...]).astype(o_ref.dtype)


def _ssd_mixer(h, j, h0, w_in, w_conv, b_conv, dt_bias, a_log, d_skip, norm_g, w_out, rows, n_prompt):
    m, d = h.shape
    nl = w_in.shape[0]
    n_seq_s, _, _, nh, hp, ns = h0.shape
    di = nh * hp
    conv_dim = w_conv.shape[-1]
    ng = (conv_dim - di) // (2 * ns)
    lc = SSM_CHUNK
    assert rows.seq_p % lc == 0 and rows.seq_s % lc == 0 and nh % ng == 0
    tm = rows.seq_s
    npt = rows.mp // tm

    tnz = _tile(di, 1024)
    tmz = _tile(rows.seq_s, 1024)
    z = _fused_matmul(h, w_in, j, [0], tm=tmz, tn=tnz, n_tiles=di // tnz, extras=[],
                      outs=[_out_tile(m, di, tmz, tnz, bf16)], epilogue=_epi_plain, name="ssd_in_z")[0]
    tnx = _tile(di, 512)
    assert di % tnx == 0 and (2 * ng * ns) % tnx == 0
    cw = w_conv.shape[1]
    bcv = b_conv.reshape(nl, 1, conv_dim)
    conv_epi = functools.partial(_conv_silu_epilogue, chunk=rows.chunk, n_prompt_tiles=npt,
                                 seq_p=rows.seq_p, seq_s=rows.seq_s)

    def conv_part(col0, width_cols, dtype, name):
        nt = width_cols // tnx
        woff = (di + col0) // tnx
        coff = col0 // tnx
        return _fused_matmul(
            h, w_in, j, [woff], tm=tm, tn=tnx, n_tiles=nt,
            extras=[(w_conv, pl.BlockSpec((None, cw, tnx), lambda jj, i: (j, 0, coff + jj))),
                    (bcv, _bias_spec(j, tnx, coff))],
            outs=[_out_tile(m, width_cols, tm, tnx, dtype)], epilogue=conv_epi, chunk=rows.chunk, name=name)[0]

    xs = conv_part(0, di, bf16, "ssd_in_x")
    bc = conv_part(di, 2 * ng * ns, bf16, "ssd_in_bc")
    ndt = 2 * nh
    dt = _fused_matmul(h, w_in, j, [(di + conv_dim) // ndt], tm=tmz, tn=ndt, n_tiles=1,
                       extras=[(dt_bias.reshape(nl, 1, ndt), _bias_spec(j, ndt))],
                       outs=[_out_tile(m, ndt, tmz, ndt, f32)], epilogue=_softplus_epilogue, name="ssd_in_dt")[0]

    tp, ts = rows.mp // lc, rows.ms // lc
    ncp, ncs = rows.seq_p // lc, rows.seq_s // lc
    nt = tp + ts
    assert lc == SLAB_LANES and ns == lc and 2 * hp == SLAB_LANES and (nh // ng) % 2 == 0 and ng % 2 == 0
    chunk = lambda dd, t: jnp.where(dd == 0, t, nt - 1 - t)
    rowmap = lambda dd, t: (chunk(dd, t), 0)
    y2, new_state = pl.pallas_call(
        functools.partial(_ssd_scan_kernel, nh=nh, hp=hp, ns=ns, ng=ng, tp=tp, ncp=ncp, ncs=ncs),
        grid=(2, nt),
        in_specs=[pl.BlockSpec((lc, di), rowmap),
                  pl.BlockSpec((lc, 2 * ng * ns), rowmap),
                  pl.BlockSpec((lc, ndt), rowmap),
                  pl.BlockSpec((None, 1, ndt), lambda dd, t: (j, 0, 0)),
                  pl.BlockSpec((None, None, None, nh, hp, ns),
                               lambda dd, t: (jnp.clip((chunk(dd, t) - tp) // ncs, 0, n_seq_s - 1), j, dd, 0, 0, 0))],
        out_specs=[pl.BlockSpec((None, lc, di), lambda dd, t: (dd, chunk(dd, t), 0)),
                   pl.BlockSpec((None, None, nh, hp, ns),
                                lambda dd, t: (jnp.clip(chunk(dd, t) // ncp, 0, n_prompt - 1), dd, 0, 0, 0))],
        out_shape=[jax.ShapeDtypeStruct((2, m, di), bf16),
                   jax.ShapeDtypeStruct((n_prompt, 2, nh, hp, ns), f32)],
        scratch_shapes=[pltpu.VMEM((ns, di), f32)] + [pltpu.VMEM((nh, lc), f32)] * 4
                       + [pltpu.VMEM((ng, lc, nh // ng), f32)],
        compiler_params=_params(2), name="ssd_scan",
    )(xs, bc, dt, a_log.reshape(nl, 1, ndt), h0)

    tg = _tile(rows.seq_s, 256)
    vrow = pl.BlockSpec((tg, di), lambda i: (i, 0))
    gated = pl.pallas_call(
        _ssd_gate_kernel, grid=(m // tg,),
        in_specs=[pl.BlockSpec((2, tg, di), lambda i: (0, i, 0)), vrow, vrow,
                  pl.BlockSpec((None, 1, di), lambda i: (j, 0, 0)),
                  pl.BlockSpec((None, 1, di), lambda i: (j, 0, 0))],
        out_specs=vrow, out_shape=jax.ShapeDtypeStruct((m, di), bf16),
        compiler_params=_params(1), name="ssd_gate",
    )(y2, xs, z, jnp.repeat(d_skip, hp, axis=1).reshape(nl, 1, di), norm_g.reshape(nl, 1, di))
    tmo, tno = _tile(m, 512), _tile(d, 512)
    out = _fused_matmul(gated, w_out, j, [0], tm=tmo, tn=tno, n_tiles=d // tno, extras=[],
                        outs=[_out_tile(m, d, tmo, tno, BRANCH_DTYPE)], epilogue=_epi_plain, name="ssd_out")[0]
    return out, new_state


def _norm_rope(x, cos_g, sin_g, even_lane):
    n = x.shape[-1]
    partner = jnp.where(even_lane, pltpu.roll(x, n - 1, axis=1), pltpu.roll(x, 1, axis=1))
    r = lax.rsqrt(jnp.mean(x * x, axis=-1, keepdims=True) + EPS)
    return (x * cos_g + partner * sin_g) * r, r


def _q_epilogue(acc_refs, e_refs, o_refs, i, *, hd):
    cos_ref, sin_ref = e_refs
    acc_ref, o_ref = acc_refs[0], o_refs[0]
    tm, tn = acc_ref.shape
    even_lane = (lax.broadcasted_iota(jnp.int32, (QK_ROWS, hd), 1) & 1) == 0

    def rows(ridx, carry):
        rs = pl.ds(pl.multiple_of(ridx * QK_ROWS, QK_ROWS), QK_ROWS)
        cos_g, sin_g = cos_ref[rs, :], sin_ref[rs, :]
        for hh in range(tn // hd):
            q, _ = _norm_rope(acc_ref[rs, hh * hd:(hh + 1) * hd], cos_g, sin_g, even_lane)
            o_ref[rs, hh * hd:(hh + 1) * hd] = q.astype(o_ref.dtype)
        return carry

    lax.fori_loop(0, tm // QK_ROWS, rows, 0)


def _kv_epilogue(acc_refs, e_refs, o_refs, i, *, hd, nkv):
    gain_ref, cos_ref, sin_ref = e_refs
    acc_ref = acc_refs[0]
    kv_ref, kv_f32_ref = o_refs
    tm = acc_ref.shape[0]
    gain = jnp.tile(jnp.broadcast_to(gain_ref[...], (ROW_HALO, hd)), (QK_ROWS // ROW_HALO, 1))
    even_lane = (lax.broadcasted_iota(jnp.int32, (QK_ROWS, hd), 1) & 1) == 0

    def rows(ridx, carry):
        rs = pl.ds(pl.multiple_of(ridx * QK_ROWS, QK_ROWS), QK_ROWS)
        cos_g, sin_g = cos_ref[rs, :], sin_ref[rs, :]
        for hh in range(nkv):
            x = acc_ref[rs, hh * hd:(hh + 1) * hd]
            k_rot, r = _norm_rope(x, cos_g, sin_g, even_lane)
            kv_f32_ref[rs, hh * hd:(hh + 1) * hd] = x * r * gain
            kv_ref[rs, hh * hd:(hh + 1) * hd] = k_rot.astype(kv_ref.dtype)
        v = acc_ref[rs, nkv * hd:]
        kv_f32_ref[rs, nkv * hd:] = v
        kv_ref[rs, nkv * hd:] = v.astype(kv_ref.dtype)
        return carry

    lax.fori_loop(0, tm // QK_ROWS, rows, 0)


def _attn_kernel(q_ref, k_ref, v_ref, *rest, group, hd, cached):
    if cached:
        ck_ref, cv_ref, o_ref = rest
        ck = ck_ref[...].astype(bf16)
        cv = cv_ref[...].astype(bf16)
    else:
        (o_ref,) = rest
    k = k_ref[...]
    v = v_ref[...]
    nt = (((1,), (1,)), ((), ()))
    for g in range(group):
        q = q_ref[:, g * hd:(g + 1) * hd]
        s = lax.dot_general(q, k, nt, preferred_element_type=f32)
        mx = jnp.max(s, axis=-1, keepdims=True)
        if cached:
            s2 = lax.dot_general(q, ck, nt, preferred_element_type=f32)
            mx = jnp.maximum(mx, jnp.max(s2, axis=-1, keepdims=True))
        p = jnp.exp(s - mx)
        den = jnp.sum(p, axis=-1, keepdims=True)
        o = jnp.dot(p.astype(bf16), v, preferred_element_type=f32)
        if cached:
            p2 = jnp.exp(s2 - mx)
            den = den + jnp.sum(p2, axis=-1, keepdims=True)
            o = o + jnp.dot(p2.astype(bf16), cv, preferred_element_type=f32)
        o_ref[:, g * hd:(g + 1) * hd] = (o / den).astype(o_ref.dtype)


def _attention(h, j, cache_k, cache_v, w_qkv, q_norm, k_norm, w_o, rows, n_prompt):
    m, d = h.shape
    nl = w_qkv.shape[0]
    hd = q_norm.shape[-1]
    n_seq_s, _, past, nkv, _ = cache_k.shape
    nq = w_o.shape[1] // hd
    group = nq // nkv
    assert rows.seq_s % GRID_W == 0
    pos = jnp.arange(rows.seq_s)
    inv = ROPE_THETA ** (-jnp.arange(hd // 4, dtype=f32) / (hd // 4))
    ang = jnp.concatenate([(pos // GRID_W).astype(f32)[:, None] * inv, (pos % GRID_W).astype(f32)[:, None] * inv], -1)
    cos_s = jnp.repeat(jnp.cos(ang), 2, axis=-1)
    sin_s = jnp.stack([-jnp.sin(ang), jnp.sin(ang)], axis=-1).reshape(rows.seq_s, hd)
    reps = rows.ms // rows.seq_s
    cos_t = jnp.concatenate([jnp.ones((rows.mp, hd), f32), jnp.tile(cos_s, (reps, 1))], 0)
    sin_t = jnp.concatenate([jnp.zeros((rows.mp, hd), f32), jnp.tile(sin_s, (reps, 1))], 0)
    pair_swap = lambda g: g.reshape(hd // 2, 2)[:, ::-1].reshape(hd)
    gq, gk = q_norm[j] * hd ** -0.5, k_norm[j]
    cos_q, sin_q = cos_t * gq, sin_t * pair_swap(gq)
    cos_k, sin_k = cos_t * gk, sin_t * pair_swap(gk)

    tm = _tile(rows.seq_s, 1024)
    tab = pl.BlockSpec((tm, hd), lambda jj, i: (i, 0))
    gspec = pl.BlockSpec((None, 1, hd), lambda jj, i: (j, 0, 0))
    nqc, nkc = nq * hd, nkv * hd
    tnq = _tile(nqc, 1024)
    q = _fused_matmul(h, w_qkv, j, [0], tm=tm, tn=tnq, n_tiles=nqc // tnq,
                      extras=[(cos_q, tab), (sin_q, tab)],
                      outs=[_out_tile(m, nqc, tm, tnq, bf16)],
                      epilogue=functools.partial(_q_epilogue, hd=hd), buffered=True, name="attn_q")[0]
    tnk = 2 * nkc
    assert nqc % tnk == 0
    kv, kv_f32 = _fused_matmul(h, w_qkv, j, [nqc // tnk], tm=tm, tn=tnk, n_tiles=1,
                               extras=[(k_norm.reshape(nl, 1, hd), gspec), (cos_k, tab), (sin_k, tab)],
                               outs=[_out_tile(m, tnk, tm, tnk, bf16), _out_tile(m, tnk, tm, tnk, f32)],
                               epilogue=functools.partial(_kv_epilogue, hd=hd, nkv=nkv), buffered=True,
                               name="attn_kv")

    gw = group * hd
    sp = rows.seq_p
    o_p = pl.pallas_call(
        functools.partial(_attn_kernel, group=group, hd=hd, cached=False),
        grid=(n_prompt, nkv),
        in_specs=[pl.BlockSpec((sp, gw), lambda b, kh: (b, kh)),
                  pl.BlockSpec((sp, hd), lambda b, kh: (b, kh)),
                  pl.BlockSpec((sp, hd), lambda b, kh: (b, nkv + kh))],
        out_specs=pl.BlockSpec((sp, gw), lambda b, kh: (b, kh)),
        out_shape=jax.ShapeDtypeStruct((rows.mp, nqc), bf16),
        compiler_params=_params(2), name="attn_prompt",
    )(q, kv, kv)
    ss = rows.seq_s
    tq = _tile(ss, 512)
    nqb = ss // tq
    pb = rows.mp // ss
    pbq = rows.mp // tq
    ck = cache_k[:, j].reshape(n_seq_s, past, nkc)
    cv = cache_v[:, j].reshape(n_seq_s, past, nkc)
    o_s = pl.pallas_call(
        functools.partial(_attn_kernel, group=group, hd=hd, cached=True),
        grid=(n_seq_s, nkv, nqb),
        in_specs=[pl.BlockSpec((tq, gw), lambda b, kh, qi: (pbq + b * nqb + qi, kh)),
                  pl.BlockSpec((ss, hd), lambda b, kh, qi: (pb + b, kh)),
                  pl.BlockSpec((ss, hd), lambda b, kh, qi: (pb + b, nkv + kh)),
                  pl.BlockSpec((None, past, hd), lambda b, kh, qi: (b, 0, kh)),
                  pl.BlockSpec((None, past, hd), lambda b, kh, qi: (b, 0, kh))],
        out_specs=pl.BlockSpec((tq, gw), lambda b, kh, qi: (b * nqb + qi, kh)),
        out_shape=jax.ShapeDtypeStruct((rows.ms, nqc), bf16),
        compiler_params=_params(3), name="attn_sample",
    )(q, kv, kv, ck, cv)
    o = jnp.concatenate([o_p, o_s], axis=0)
    tmo, tno = _tile(m, 1024), _tile(d, 1024)
    out = _fused_matmul(o, w_o, j, [0], tm=tmo, tn=tno, n_tiles=d // tno, extras=[],
                        outs=[_out_tile(m, d, tmo, tno, BRANCH_DTYPE)], epilogue=_epi_plain, name="attn_o")[0]
    new_k = kv_f32[:rows.mp, :nkc].reshape(n_prompt, sp, nkv, hd)
    new_v = kv_f32[:rows.mp, nkc:].reshape(n_prompt, sp, nkv, hd)
    return out, new_k, new_v


def kernel(x_prompt, x_sample, c, state_ssd, cache_k, cache_v, c_ctx, w_mod, b_mod, norm_pre, norm_post, cv_w_pw1, cv_b_pw1, cv_w_dw, cv_b_dw, cv_ln_g, cv_ln_b, cv_w_pw2, cv_b_pw2, ssd_w_in, ssd_w_conv, ssd_b_conv, ssd_dt_bias, ssd_a_log, ssd_d, ssd_norm_g, ssd_w_out, attn_w_qkv, attn_q_norm, attn_k_norm, attn_w_o, ffn_w_up, ffn_w_dw, ffn_b_dw, ffn_w_down):
    nb, sp, d = x_prompt.shape
    ndb, ss, _ = x_sample.shape
    depth = w_mod.shape[0]
    rows = _Rows(nb * sp, ndb * ss, sp, ss)
    assert sp & (sp - 1) == 0 and ss & (ss - 1) == 0, "sequence lengths must be powers of two"
    assert rows.mp % ss == 0 and 1 + ndb <= COND_ROWS

    y = (x_prompt.reshape(rows.mp, d), x_sample.reshape(rows.ms, d))
    cond = jnp.concatenate([c_ctx[None, :], c, jnp.zeros((COND_ROWS - 1 - ndb, d), f32)], axis=0)
    mod = _modulation(cond, w_mod, b_mod).reshape(depth * COND_ROWS, 6, d)
    gpre = norm_pre.reshape(depth * 2, 1, d)
    gpost = norm_post.reshape(depth * 2, 1, d)
    h = _prenorm(y, gpre, 0, mod, 0, 1, 0, rows)
    new_ssd, new_k, new_v = [], [], []
    for i in range(depth):
        kind, j = i % N_MIXERS, i // N_MIXERS
        if kind == 0:
            o = _conformer(h, j, cv_w_pw1, cv_b_pw1, cv_w_dw, cv_b_dw, cv_ln_g, cv_ln_b, cv_w_pw2, cv_b_pw2, rows)
        elif kind == 1:
            o, st = _ssd_mixer(h, j, state_ssd, ssd_w_in, ssd_w_conv, ssd_b_conv, ssd_dt_bias, ssd_a_log,
                               ssd_d, ssd_norm_g, ssd_w_out, rows, nb)
            new_ssd.append(st)
        else:
            o, k_new, v_new = _attention(h, j, cache_k, cache_v, attn_w_qkv, attn_q_norm, attn_k_norm,
                                         attn_w_o, rows, nb)
            new_k.append(k_new)
            new_v.append(v_new)
        y, h = _resid(y, o, gpost, 2 * i, mod, i, 2, rows, nxt=(gpre, 2 * i + 1, i, 4, 3))
        o = _conv_ffn(h, i, ffn_w_up, ffn_w_dw, ffn_b_dw, ffn_w_down, rows)
        if i + 1 < depth:
            y, h = _resid(y, o, gpost, 2 * i + 1, mod, i, 5, rows, nxt=(gpre, 2 * i + 2, i + 1, 1, 0))
        else:
            yp, ys = _resid(y, o, gpost, 2 * i + 1, mod, i, 5, rows, split_out=True)
    yp = yp.reshape(nb, sp, d)
    ys = ys.reshape(ndb, ss, d)
    return (yp, ys, jnp.stack(new_ssd, axis=1), jnp.stack(new_k, axis=1), jnp.stack(new_v, axis=1))
```

```python
import functools
import math

import jax
import jax.numpy as jnp
from jax import lax
from jax.experimental import pallas as pl
from jax.experimental.pallas import tpu as pltpu

f32 = jnp.float32
bf16 = jnp.bfloat16
BRANCH_DTYPE = bf16

EPS = 1e-6
GRID_W = 64
ROPE_THETA = 10000.0
SSM_CHUNK = 128
N_MIXERS = 3
V7X_VMEM_LIMIT_BYTES = 56 * 1024 * 1024
COND_ROWS = 8
ROW_HALO = 8
ROW_CHUNK = 256
CONV_BLOCK = 256
SLAB_LANES = 128
EPILOGUE_LAG = 1


def _params(n_axes, vmem=V7X_VMEM_LIMIT_BYTES):
    return pltpu.CompilerParams(dimension_semantics=("arbitrary",) * n_axes,
                                vmem_limit_bytes=vmem)


def _tile(dim, pref):
    return pref if dim % pref == 0 else dim


def _sigmoid(x):
    return 1.0 / (1.0 + jnp.exp(-x))


def _silu(x):
    return x * _sigmoid(x)


def _rms(x, g):
    return x * lax.rsqrt(jnp.mean(x * x, axis=-1, keepdims=True) + EPS) * g


def _mod_kernel(c_ref, w_ref, b_ref, o_ref):
    s = _silu(c_ref[...]).astype(bf16)
    o_ref[...] = jnp.dot(s, w_ref[...].astype(bf16), preferred_element_type=f32) + b_ref[...]


def _modulation(cond, w_mod, b_mod):
    depth, d, n = w_mod.shape
    tn = _tile(n, 1024)
    return pl.pallas_call(
        _mod_kernel,
        grid=(depth, n // tn),
        in_specs=[pl.BlockSpec((COND_ROWS, d), lambda l, j: (0, 0)),
                  pl.BlockSpec((None, d, tn), lambda l, j: (l, 0, j)),
                  pl.BlockSpec((None, 1, tn), lambda l, j: (l, 0, j))],
        out_specs=pl.BlockSpec((None, COND_ROWS, tn), lambda l, j: (l, 0, j)),
        out_shape=jax.ShapeDtypeStruct((depth, COND_ROWS, n), f32),
        compiler_params=_params(2), name="modulation",
    )(cond, w_mod, b_mod.reshape(depth, 1, n))


def _prenorm_kernel(yp_ref, ys_ref, g_ref, mod_ref, h_ref, *, sc, sh, n_prompt_tiles):
    y = jnp.where(pl.program_id(0) < n_prompt_tiles, yp_ref[...], ys_ref[...])
    yn = _rms(y, g_ref[...])
    h_ref[...] = (yn * (1.0 + mod_ref[sc:sc + 1, :]) + mod_ref[sh:sh + 1, :]).astype(h_ref.dtype)


def _resid_kernel(*refs, ga, nxt, split_in, split_out, n_prompt_tiles):
    i = pl.program_id(0)
    if split_in:
        yp_ref, ys_ref, o_ref, gpost_ref, moda_ref, *rest = refs
        y_in = jnp.where(i < n_prompt_tiles, yp_ref[...], ys_ref[...])
    else:
        y_ref, o_ref, gpost_ref, moda_ref, *rest = refs
        y_in = y_ref[...]
    y = y_in + moda_ref[ga:ga + 1, :] * _rms(o_ref[...].astype(f32), gpost_ref[...])
    if split_out:
        outp_ref, outs_ref = rest

        @pl.when(i < n_prompt_tiles)
        def _():
            outp_ref[...] = y

        @pl.when(i >= n_prompt_tiles)
        def _():
            outs_ref[...] = y
    elif nxt is None:
        (ynew_ref,) = rest
        ynew_ref[...] = y
    else:
        gpre_ref, modb_ref, ynew_ref, h_ref = rest
        sc, sh = nxt
        ynew_ref[...] = y
        yn = _rms(y, gpre_ref[...])
        h_ref[...] = (yn * (1.0 + modb_ref[sc:sc + 1, :]) + modb_ref[sh:sh + 1, :]).astype(h_ref.dtype)


class _Rows:
    def __init__(self, mp, ms, seq_p, seq_s):
        self.mp, self.ms, self.m = mp, ms, mp + ms
        self.seq_p, self.seq_s = seq_p, seq_s
        self.chunk = _tile(seq_p, ROW_CHUNK)
        assert seq_p % self.chunk == 0 and seq_s % self.chunk == 0

    def cond_of_tile(self, tm):
        assert self.mp % tm == 0 and self.seq_s % tm == 0
        npt, per = self.mp // tm, self.seq_s // tm
        return lambda i: jnp.where(i < npt, 0, 1 + (i - npt) // per)


def _split_specs(rows, tm, d):
    npt = rows.mp // tm
    return [pl.BlockSpec((tm, d), lambda i: (jnp.minimum(i, npt - 1), 0)),
            pl.BlockSpec((tm, d), lambda i: (jnp.maximum(i - npt, 0), 0))]


def _prenorm(y_pair, g_all, g_idx, mod, layer, sc, sh, rows):
    d = y_pair[0].shape[1]
    m = rows.m
    tm = _tile(rows.seq_s, 512)
    cond = rows.cond_of_tile(tm)
    return pl.pallas_call(
        functools.partial(_prenorm_kernel, sc=sc, sh=sh, n_prompt_tiles=rows.mp // tm),
        grid=(m // tm,),
        in_specs=_split_specs(rows, tm, d)
                 + [pl.BlockSpec((None, 1, d), lambda i: (g_idx, 0, 0)),
                    pl.BlockSpec((None, 6, d), lambda i: (layer * COND_ROWS + cond(i), 0, 0))],
        out_specs=pl.BlockSpec((tm, d), lambda i: (i, 0)),
        out_shape=jax.ShapeDtypeStruct((m, d), bf16),
        compiler_params=_params(1), name="prenorm",
    )(*y_pair, g_all, mod)


def _resid(y, o, gpost_all, gpost_idx, mod, layer, ga, rows, nxt=None, split_out=False):
    m, d = o.shape
    tm = _tile(rows.seq_s, 512)
    cond = rows.cond_of_tile(tm)
    row_spec = pl.BlockSpec((tm, d), lambda i: (i, 0))
    split_in = isinstance(y, tuple)
    in_specs = (_split_specs(rows, tm, d) if split_in else [row_spec]) + [
        row_spec,
        pl.BlockSpec((None, 1, d), lambda i: (gpost_idx, 0, 0)),
        pl.BlockSpec((None, 6, d), lambda i: (layer * COND_ROWS + cond(i), 0, 0))]
    args = (list(y) if split_in else [y]) + [o, gpost_all, mod]
    if split_out:
        assert nxt is None
        out_specs = _split_specs(rows, tm, d)
        out_shape = [jax.ShapeDtypeStruct((rows.mp, d), f32), jax.ShapeDtypeStruct((rows.ms, d), f32)]
        kn = None
    elif nxt is None:
        out_specs, out_shape, kn = row_spec, jax.ShapeDtypeStruct((m, d), f32), None
    else:
        gpre_all, gpre_idx, nlayer, sc, sh = nxt
        in_specs += [pl.BlockSpec((None, 1, d), lambda i: (gpre_idx, 0, 0)),
                     pl.BlockSpec((None, 6, d), lambda i: (nlayer * COND_ROWS + cond(i), 0, 0))]
        args += [gpre_all, mod]
        out_specs = [row_spec, row_spec]
        out_shape = [jax.ShapeDtypeStruct((m, d), f32), jax.ShapeDtypeStruct((m, d), bf16)]
        kn = (sc, sh)
    return pl.pallas_call(
        functools.partial(_resid_kernel, ga=ga, nxt=kn, split_in=split_in, split_out=split_out,
                          n_prompt_tiles=rows.mp // tm),
        grid=(m // tm,), in_specs=in_specs, out_specs=out_specs, out_shape=out_shape,
        compiler_params=_params(1), name="resid",
    )(*args)


def _fmm_kernel(*refs, n_slab, n_extra, n_out, epilogue, chunk):
    x_ref = refs[0]
    w_refs = refs[1:1 + n_slab]
    e_refs = refs[1 + n_slab:1 + n_slab + n_extra]
    o_refs = refs[1 + n_slab + n_extra:1 + n_slab + n_extra + n_out]
    wb_refs = refs[1 + n_slab + n_extra + n_out:1 + 2 * n_slab + n_extra + n_out]
    u_refs = refs[1 + 2 * n_slab + n_extra + n_out:]
    i = pl.program_id(1)

    @pl.when(i == 0)
    def _():
        for w_ref, wb_ref in zip(w_refs, wb_refs):
            wb_ref[...] = w_ref[...].astype(bf16)

    if chunk is None:
        x = x_ref[...]
        accs = [jnp.dot(x, wb_ref[...], preferred_element_type=f32) for wb_ref in wb_refs]
        if u_refs:
            for u_ref, acc in zip(u_refs, accs):
                u_ref[...] = acc
            accs = u_refs
        epilogue(accs, e_refs, o_refs, i)
        return

    tm = x_ref.shape[0]
    n_chunk = tm // chunk
    per_slab = [u_refs[s * n_chunk:(s + 1) * n_chunk] for s in range(n_slab)]
    zeros = jnp.zeros((ROW_HALO, u_refs[0].shape[1]), f32)
    for bufs in per_slab:
        bufs[0][0:ROW_HALO, :] = zeros
        bufs[-1][ROW_HALO + chunk:, :] = zeros
    for c in range(n_chunk + EPILOGUE_LAG):
        if c < n_chunk:
            xc = x_ref[c * chunk:(c + 1) * chunk, :]
            for wb_ref, bufs in zip(wb_refs, per_slab):
                prod = jnp.dot(xc, wb_ref[...], preferred_element_type=f32)
                bufs[c][ROW_HALO:ROW_HALO + chunk, :] = prod
                if c > 0:
                    bufs[c - 1][ROW_HALO + chunk:, :] = prod[0:ROW_HALO]
                if c + 1 < n_chunk:
                    bufs[c + 1][0:ROW_HALO, :] = prod[chunk - ROW_HALO:]
        if c >= EPILOGUE_LAG:
            done = c - EPILOGUE_LAG
            epilogue(done, [bufs[done] for bufs in per_slab], e_refs, o_refs, i)


def _fused_matmul(x, w_all, layer, slab_offs, *, tm, tn, n_tiles, extras, outs, epilogue, name, chunk=None,
                  buffered=False):
    m, k = x.shape
    in_specs = [pl.BlockSpec((tm, k), lambda j, i: (i, 0))]
    args = [x]
    for off in slab_offs:
        in_specs.append(pl.BlockSpec((None, k, tn), lambda j, i, off=off: (layer, 0, off + j)))
        args.append(w_all)
    for arr, spec in extras:
        in_specs.append(spec)
        args.append(arr)
    scratch = [pltpu.VMEM((k, tn), bf16) for _ in slab_offs]
    if chunk is not None:
        assert tm % chunk == 0 and not buffered
        scratch += [pltpu.VMEM((chunk + 2 * ROW_HALO, tn), f32)] * (len(slab_offs) * (tm // chunk))
    if buffered:
        scratch += [pltpu.VMEM((tm, tn), f32) for _ in slab_offs]
    return pl.pallas_call(
        functools.partial(_fmm_kernel, n_slab=len(slab_offs), n_extra=len(extras),
                          n_out=len(outs), epilogue=epilogue, chunk=chunk),
        grid=(n_tiles, m // tm),
        in_specs=in_specs,
        out_specs=[spec for _, spec in outs],
        out_shape=[sds for sds, _ in outs],
        scratch_shapes=scratch,
        compiler_params=_params(2), name=name,
    )(*args)


def _bias_spec(layer, tn, off=0):
    return pl.BlockSpec((None, 1, tn), lambda j, i: (layer, 0, off + j))


def _out_tile(m, n, tm, tn, dtype):
    return (jax.ShapeDtypeStruct((m, n), dtype), pl.BlockSpec((tm, tn), lambda j, i: (i, j)))


def _epi_plain(accs, e_refs, o_refs, i):
    o_refs[0][...] = accs[0].astype(o_refs[0].dtype)


def _epi_bias(accs, e_refs, o_refs, i):
    o_refs[0][...] = (accs[0] + e_refs[0][...]).astype(o_refs[0].dtype)


class _ChunkConv:
    def __init__(self, u_ref, w, bias, c, chunk, seq):
        self.u_ref, self.chunk = u_ref, chunk
        self.width = w.shape[0]
        self.left = (self.width - 1) // 2
        assert self.left <= ROW_HALO and self.width - 1 - self.left <= ROW_HALO
        assert chunk % CONV_BLOCK == 0 and CONV_BLOCK >= 2 * ROW_HALO
        tn = u_ref.shape[1]
        self.starts_seq = ((c * chunk) & (seq - 1)) == 0
        self.ends_seq = (((c + 1) * chunk) & (seq - 1)) == 0
        self.w_tile = [jnp.broadcast_to(w[k:k + 1, :], (ROW_HALO, tn)) for k in range(self.width)]
        self.w_block = [jnp.tile(wt, (CONV_BLOCK // ROW_HALO, 1)) for wt in self.w_tile]
        self.b_tile = jnp.broadcast_to(bias, (ROW_HALO, tn))
        self.b_block = jnp.tile(self.b_tile, (CONV_BLOCK // ROW_HALO, 1))

    def _taps(self, r0, n_rows, masked):
        w_rows = self.w_tile if n_rows == ROW_HALO else self.w_block
        acc = self.b_tile if n_rows == ROW_HALO else self.b_block
        far = self.chunk + ROW_HALO
        if masked:
            row = lax.broadcasted_iota(jnp.int32, acc.shape, 0) + r0
        for k in range(self.width):
            d = k - self.left
            v = self.u_ref[pl.ds(ROW_HALO + r0 + d, n_rows), :]
            if masked and d < 0:
                v = jnp.where(row >= jnp.where(self.starts_seq, -d, -far), v, 0.0)
            if masked and d > 0:
                v = jnp.where(row < jnp.where(self.ends_seq, self.chunk - d, far), v, 0.0)
            acc = acc + v * w_rows[k]
        return acc

    def block(self, r0):
        out = self._taps(r0, CONV_BLOCK, False)
        if r0 == 0:
            out = jnp.concatenate([self._taps(0, ROW_HALO, True), out[ROW_HALO:]], axis=0)
        if r0 + CONV_BLOCK == self.chunk:
            tail = self._taps(self.chunk - ROW_HALO, ROW_HALO, True)
            out = jnp.concatenate([out[:CONV_BLOCK - ROW_HALO], tail], axis=0)
        return out


def _ffn_up_epilogue(c, u_refs, e_refs, o_refs, i, *, chunk, n_prompt_tiles, seq_p, seq_s):
    wa_ref, wg_ref, ba_ref, bg_ref = e_refs
    seq = jnp.where(i < n_prompt_tiles, seq_p, seq_s)
    conv_a = _ChunkConv(u_refs[0], wa_ref[...], ba_ref[...], c, chunk, seq)
    conv_g = _ChunkConv(u_refs[1], wg_ref[...], bg_ref[...], c, chunk, seq)
    for r0 in range(0, chunk, CONV_BLOCK):
        a, g = conv_a.block(r0), conv_g.block(r0)
        o_refs[0][c * chunk + r0:c * chunk + r0 + CONV_BLOCK, :] = (_silu(g) * a).astype(o_refs[0].dtype)


def _conv_ffn(h, layer, w_up, w_dw, b_dw, w_down, rows):
    m, d = h.shape
    dff = w_down.shape[1]
    depth = w_up.shape[0]
    tm = rows.seq_s
    tn = _tile(dff, 512)
    nt = dff // tn
    width = w_dw.shape[1]
    b3 = b_dw.reshape(depth, 1, 2 * dff)
    dw_spec = lambda off: pl.BlockSpec((None, width, tn), lambda j, i: (layer, 0, off + j))
    act = _fused_matmul(
        h, w_up, layer, [0, nt], tm=tm, tn=tn, n_tiles=nt,
        extras=[(w_dw, dw_spec(0)), (w_dw, dw_spec(nt)), (b3, _bias_spec(layer, tn)), (b3, _bias_spec(layer, tn, nt))],
        outs=[_out_tile(m, dff, tm, tn, bf16)], chunk=rows.chunk,
        epilogue=functools.partial(_ffn_up_epilogue, chunk=rows.chunk, n_prompt_tiles=rows.mp // tm,
                                   seq_p=rows.seq_p, seq_s=rows.seq_s),
        name="ffn_up")[0]
    tm2, tn2 = _tile(m, 512), _tile(d, 512)
    return _fused_matmul(act, w_down, layer, [0], tm=tm2, tn=tn2, n_tiles=d // tn2, extras=[],
                         outs=[_out_tile(m, d, tm2, tn2, BRANCH_DTYPE)], epilogue=_epi_plain, name="ffn_down")[0]


def _glu_epilogue(accs, e_refs, o_refs, i):
    a = accs[0] + e_refs[0][...]
    g = accs[1] + e_refs[1][...]
    o_refs[0][...] = (a * _sigmoid(g)).astype(o_refs[0].dtype)


CONV_HALO = 16
CONV_LANES = 128
CONV_ROWS = 128
LN_ROWS = 32
QK_ROWS = 128


def _cv_mid_kernel(u_ref, prev_ref, next_ref, w_ref, b_ref, g_ref, beta_ref, o_ref, pad_ref, conv_ref,
                   *, blocks_p, per_seq_p, per_seq_s):
    i = pl.program_id(0)
    rb, d = u_ref.shape
    width = w_ref.shape[0]
    left = (width - 1) // 2
    per = jnp.where(i < blocks_p, per_seq_p, per_seq_s)
    k_in_seq = jnp.where(i < blocks_p, i, i - blocks_p) % per
    first = k_in_seq == 0
    last = k_in_seq == per - 1
    pad_ref[0:CONV_HALO, :] = jnp.where(first, 0.0, prev_ref[...])
    pad_ref[CONV_HALO:CONV_HALO + rb, :] = u_ref[...]
    pad_ref[CONV_HALO + rb:CONV_HALO + rb + CONV_HALO, :] = jnp.where(last, 0.0, next_ref[...])

    def lane_chunk(cidx, carry):
        lo = pl.multiple_of(cidx * CONV_LANES, CONV_LANES)
        wc = w_ref[:, pl.ds(lo, CONV_LANES)]
        for r0 in range(0, rb, CONV_ROWS):
            partial = []
            for phase in range(ROW_HALO):
                ks = [k for k in range(width) if (CONV_HALO - left + k) % ROW_HALO == phase]
                if not ks:
                    continue
                span = max(CONV_HALO - left + k for k in ks) - phase
                shifted = pad_ref[pl.ds(r0 + phase, CONV_ROWS + span), pl.ds(lo, CONV_LANES)]
                acc = None
                for k in ks:
                    off = CONV_HALO - left + k - phase
                    term = shifted[off:off + CONV_ROWS] * wc[k:k + 1, :]
                    acc = term if acc is None else acc + term
                partial.append(acc)
            total = partial[0]
            for acc in partial[1:]:
                total = total + acc
            conv_ref[pl.ds(r0, CONV_ROWS), pl.ds(lo, CONV_LANES)] = total
        return carry

    lax.fori_loop(0, d // CONV_LANES, lane_chunk, 0)

    def norm_rows(ridx, carry):
        rs = pl.ds(pl.multiple_of(ridx * LN_ROWS, LN_ROWS), LN_ROWS)
        u = conv_ref[rs, :] + b_ref[...]
        mu = jnp.mean(u, axis=-1, keepdims=True)
        uc = u - mu
        v = uc * lax.rsqrt(jnp.mean(uc * uc, axis=-1, keepdims=True) + EPS) * g_ref[...] + beta_ref[...]
        o_ref[rs, :] = _silu(v).astype(o_ref.dtype)
        return carry

    lax.fori_loop(0, rb // LN_ROWS, norm_rows, 0, unroll=2)


def _conformer(h, j, w_pw1, b_pw1, w_dw, b_dw, ln_g, ln_b, w_pw2, b_pw2, rows):
    m, d = h.shape
    nl = w_pw1.shape[0]
    tm, tn = _tile(rows.seq_s, 1024), _tile(d, 512)
    nt = d // tn
    b1 = b_pw1.reshape(nl, 1, 2 * d)
    glu = _fused_matmul(h, w_pw1, j, [0, nt], tm=tm, tn=tn, n_tiles=nt,
                        extras=[(b1, _bias_spec(j, tn)), (b1, _bias_spec(j, tn, nt))],
                        outs=[_out_tile(m, d, tm, tn, f32)], epilogue=_glu_epilogue, name="cv_pw1")[0]
    width = w_dw.shape[1]
    assert (width - 1) // 2 <= CONV_HALO
    rb = _tile(rows.seq_p, 256)
    assert rows.seq_p % rb == 0 and rows.seq_s % rb == 0 and rb % CONV_ROWS == 0 and d % CONV_LANES == 0
    hb = rb // CONV_HALO
    n_halo = m // CONV_HALO
    vec = lambda a: a.reshape(nl, 1, d)
    vspec = pl.BlockSpec((None, 1, d), lambda i: (j, 0, 0))
    mid = pl.pallas_call(
        functools.partial(_cv_mid_kernel, blocks_p=rows.mp // rb, per_seq_p=rows.seq_p // rb,
                          per_seq_s=rows.seq_s // rb),
        grid=(m // rb,),
        in_specs=[pl.BlockSpec((rb, d), lambda i: (i, 0)),
                  pl.BlockSpec((CONV_HALO, d), lambda i: (jnp.maximum(i * hb - 1, 0), 0)),
                  pl.BlockSpec((CONV_HALO, d), lambda i: (jnp.minimum((i + 1) * hb, n_halo - 1), 0)),
                  pl.BlockSpec((None, width, d), lambda i: (j, 0, 0)),
                  vspec, vspec, vspec],
        out_specs=pl.BlockSpec((rb, d), lambda i: (i, 0)),
        out_shape=jax.ShapeDtypeStruct((m, d), bf16),
        scratch_shapes=[pltpu.VMEM((rb + 2 * CONV_HALO, d), f32), pltpu.VMEM((rb, d), f32)],
        compiler_params=_params(1), name="cv_mid",
    )(glu, glu, glu, w_dw, vec(b_dw), vec(ln_g), vec(ln_b))
    tm2, tn2 = _tile(m, 1024), _tile(d, 1024)
    return _fused_matmul(mid, w_pw2, j, [0], tm=tm2, tn=tn2, n_tiles=d // tn2,
                         extras=[(vec(b_pw2), _bias_spec(j, tn2))],
                         outs=[_out_tile(m, d, tm2, tn2, BRANCH_DTYPE)], epilogue=_epi_bias, name="cv_pw2")[0]


def _conv_silu_epilogue(c, u_refs, e_refs, o_refs, i, *, chunk, n_prompt_tiles, seq_p, seq_s):
    w_ref, b_ref = e_refs
    seq = jnp.where(i < n_prompt_tiles, seq_p, seq_s)
    conv = _ChunkConv(u_refs[0], w_ref[...], b_ref[...], c, chunk, seq)
    for r0 in range(0, chunk, CONV_BLOCK):
        o_refs[0][c * chunk + r0:c * chunk + r0 + CONV_BLOCK, :] = _silu(conv.block(r0)).astype(o_refs[0].dtype)


def _softplus_epilogue(accs, e_refs, o_refs, i):
    x = accs[0] + e_refs[0][...]
    o_refs[0][...] = jnp.maximum(x, 0.0) + jnp.log1p(jnp.exp(-jnp.abs(x)))


def _ssd_scan_kernel(xs_ref, bc_ref, dt_ref, alog_ref, h0_ref, y_ref, st_ref,
                     state_ref, acst_ref, dtt_ref, wt_ref, cdt_ref, acsc_ref,
                     *, nh, hp, ns, ng, tp, ncp, ncs):
    d = pl.program_id(0)
    t = pl.program_id(1)
    nt = pl.num_programs(1)
    lc = xs_ref.shape[0]
    r = nh // ng
    gp = r * hp
    fwd = d == 0
    te = jnp.where(fwd, t, nt - 1 - t)
    in_prompt = te < tp
    c_in_seq = jnp.where(in_prompt, te % ncp, (te - tp) % ncs)
    n_in_seq = jnp.where(in_prompt, ncp, ncs)
    is_first = c_in_seq == jnp.where(fwd, 0, n_in_seq - 1)
    is_last = c_in_seq == jnp.where(fwd, n_in_seq - 1, 0)

    @pl.when(jnp.logical_and(is_first, in_prompt))
    def _():
        state_ref[...] = jnp.zeros_like(state_ref)

    @pl.when(jnp.logical_and(is_first, jnp.logical_not(in_prompt)))
    def _():
        for g in range(ng):
            state_ref[:, g * gp:(g + 1) * gp] = h0_ref[g * r:(g + 1) * r].reshape(gp, ns).T

    dt2 = dt_ref[...]
    da2 = dt2 * (-jnp.exp(alog_ref[...]))
    row = lax.broadcasted_iota(jnp.int32, (lc, lc), 0)
    col = lax.broadcasted_iota(jnp.int32, (lc, lc), 1)
    diff = jnp.where(fwd, row - col, col - row)
    mask = diff >= 0
    tri = jnp.where(mask, 1.0, 0.0).astype(bf16)
    hi = da2.astype(bf16)
    rest = da2 - hi.astype(f32)
    mid = rest.astype(bf16)
    low = (rest - mid.astype(f32)).astype(bf16)
    acs2 = (jnp.dot(tri, hi, preferred_element_type=f32) + jnp.dot(tri, mid, preferred_element_type=f32)
            + jnp.dot(tri, low, preferred_element_type=f32))
    acst2 = acs2.T
    dtt2 = dt2.T
    acs = jnp.where(fwd, acs2[:, :nh], acs2[:, nh:])
    acst = jnp.where(fwd, acst2[:nh], acst2[nh:])
    dtt = jnp.where(fwd, dtt2[:nh], dtt2[nh:])
    total = jnp.where(fwd, acst[:, lc - 1:lc], acst[:, 0:1])
    totb = jnp.broadcast_to(total, (nh, lc))
    acst_ref[...] = acst
    dtt_ref[...] = dtt
    wt_ref[...] = dtt * jnp.exp(totb - acst)
    cdt_ref[...] = jnp.exp(totb)
    for g in range(ng):
        acsc_ref[g] = acs[:, g * r:(g + 1) * r]
    pair = SLAB_LANES // hp
    first_lanes = lax.broadcasted_iota(jnp.int32, (1, SLAB_LANES), 1) < hp

    def group(g, carry):
        lo = pl.multiple_of(g * gp, gp)
        bg = bc_ref[:, pl.ds(pl.multiple_of(g * ns, ns), ns)]
        cg = bc_ref[:, pl.ds(pl.multiple_of((ng + g) * ns, ns), ns)]
        scores = lax.dot_general(cg, bg, (((1,), (1,)), ((), ())), preferred_element_type=f32)
        bt = bg.astype(f32).T
        cf = cg.astype(f32)
        cols = acsc_ref[g]
        hrow = pl.ds(pl.multiple_of(g * r, r), r)
        rows_a, rows_dt, rows_w, rows_cd = acst_ref[hrow, :], dtt_ref[hrow, :], wt_ref[hrow, :], cdt_ref[hrow, :]
        glanes = pl.ds(lo, gp)
        x_g = xs_ref[:, glanes].astype(bf16)
        st_g = state_ref[:, glanes]
        ys, sts = [], []
        for k in range(r // pair):
            st = st_g[:, k * SLAB_LANES:(k + 1) * SLAB_LANES]
            x_k = x_g[:, k * SLAB_LANES:(k + 1) * SLAB_LANES]
            tops, bws = [], []
            for q in (pair * k, pair * k + 1):
                a_l = jnp.broadcast_to(cols[:, q:q + 1], (lc, lc))
                seg = a_l - rows_a[q:q + 1, :]
                mh = scores * jnp.where(mask, jnp.exp(seg), 0.0) * rows_dt[q:q + 1, :]
                ch = cf * jnp.exp(a_l)
                tops.append(jnp.concatenate([mh.astype(bf16), ch.astype(bf16)], axis=1))
                bws.append((bt * rows_w[q:q + 1, :]).astype(bf16))
            out = jnp.dot(jnp.concatenate(tops, axis=0), jnp.concatenate([x_k, st.astype(bf16)], axis=0),
                          preferred_element_type=f32)
            y = jnp.where(first_lanes, out[:lc], out[lc:])
            x_0 = jnp.where(first_lanes, x_k, jnp.zeros_like(x_k))
            x_1 = jnp.where(first_lanes, jnp.zeros_like(x_k), x_k)
            cs = jnp.dot(jnp.concatenate(bws, axis=1), jnp.concatenate([x_0, x_1], axis=0),
                         preferred_element_type=f32)
            cd = jnp.where(first_lanes, rows_cd[pair * k:pair * k + 1, :], rows_cd[pair * k + 1:pair * k + 2, :])
            sts.append(st * cd + cs)
            ys.append(y.astype(y_ref.dtype))
        state_ref[:, glanes] = jnp.concatenate(sts, axis=1)
        y_ref[:, glanes] = jnp.concatenate(ys, axis=1)
        return carry

    lax.fori_loop(0, ng, group, 0, unroll=8)

    @pl.when(jnp.logical_and(is_last, in_prompt))
    def _():
        for g in range(ng):
            st_ref[g * r:(g + 1) * r] = state_ref[:, g * gp:(g + 1) * gp].T.reshape(r, hp, ns)


def _ssd_gate_kernel(y_ref, xs_ref, z_ref, dx_ref, g_ref, o_ref):
    y = y_ref[0].astype(f32) + y_ref[1].astype(f32) + dx_ref[...] * xs_ref[...].astype(f32)
    y = y * _silu(z_ref[...].astype(f32))
    o_ref[...] = _rms(y, g_ref[...]).astype(o_ref.dtype)


def _ssd_mixer(h, j, h0, w_in, w_conv, b_conv, dt_bias, a_log, d_skip, norm_g, w_out, rows, n_prompt):
    m, d = h.shape
    nl = w_in.shape[0]
    n_seq_s, _, _, nh, hp, ns = h0.shape
    di = nh * hp
    conv_dim = w_conv.shape[-1]
    ng = (conv_dim - di) // (2 * ns)
    lc = SSM_CHUNK
    assert rows.seq_p % lc == 0 and rows.seq_s % lc == 0 and nh % ng == 0
    tm = rows.seq_s
    npt = rows.mp // tm

    tnz = _tile(di, 1024)
    tmz = _tile(rows.seq_s, 1024)
    z = _fused_matmul(h, w_in, j, [0], tm=tmz, tn=tnz, n_tiles=di // tnz, extras=[],
                      outs=[_out_tile(m, di, tmz, tnz, bf16)], epilogue=_epi_plain, name="ssd_in_z")[0]
    tnx = _tile(di, 512)
    assert di % tnx == 0 and (2 * ng * ns) % tnx == 0
    cw = w_conv.shape[1]
    bcv = b_conv.reshape(nl, 1, conv_dim)
    conv_epi = functools.partial(_conv_silu_epilogue, chunk=rows.chunk, n_prompt_tiles=npt,
                                 seq_p=rows.seq_p, seq_s=rows.seq_s)

    def conv_part(col0, width_cols, dtype, name):
        nt = width_cols // tnx
        woff = (di + col0) // tnx
        coff = col0 // tnx
        return _fused_matmul(
            h, w_in, j, [woff], tm=tm, tn=tnx, n_tiles=nt,
            extras=[(w_conv, pl.BlockSpec((None, cw, tnx), lambda jj, i: (j, 0, coff + jj))),
                    (bcv, _bias_spec(j, tnx, coff))],
            outs=[_out_tile(m, width_cols, tm, tnx, dtype)], epilogue=conv_epi, chunk=rows.chunk, name=name)[0]

    xs = conv_part(0, di, bf16, "ssd_in_x")
    bc = conv_part(di, 2 * ng * ns, bf16, "ssd_in_bc")
    ndt = 2 * nh
    dt = _fused_matmul(h, w_in, j, [(di + conv_dim) // ndt], tm=tmz, tn=ndt, n_tiles=1,
                       extras=[(dt_bias.reshape(nl, 1, ndt), _bias_spec(j, ndt))],
                       outs=[_out_tile(m, ndt, tmz, ndt, f32)], epilogue=_softplus_epilogue, name="ssd_in_dt")[0]

    tp, ts = rows.mp // lc, rows.ms // lc
    ncp, ncs = rows.seq_p // lc, rows.seq_s // lc
    nt = tp + ts
    assert lc == SLAB_LANES and ns == lc and 2 * hp == SLAB_LANES and (nh // ng) % 2 == 0 and ng % 2 == 0
    chunk = lambda dd, t: jnp.where(dd == 0, t, nt - 1 - t)
    rowmap = lambda dd, t: (chunk(dd, t), 0)
    y2, new_state = pl.pallas_call(
        functools.partial(_ssd_scan_kernel, nh=nh, hp=hp, ns=ns, ng=ng, tp=tp, ncp=ncp, ncs=ncs),
        grid=(2, nt),
        in_specs=[pl.BlockSpec((lc, di), rowmap),
                  pl.BlockSpec((lc, 2 * ng * ns), rowmap),
                  pl.BlockSpec((lc, ndt), rowmap),
                  pl.BlockSpec((None, 1, ndt), lambda dd, t: (j, 0, 0)),
                  pl.BlockSpec((None, None, None, nh, hp, ns),
                               lambda dd, t: (jnp.clip((chunk(dd, t) - tp) // ncs, 0, n_seq_s - 1), j, dd, 0, 0, 0))],
        out_specs=[pl.BlockSpec((None, lc, di), lambda dd, t: (dd, chunk(dd, t), 0)),
                   pl.BlockSpec((None, None, nh, hp, ns),
                                lambda dd, t: (jnp.clip(chunk(dd, t) // ncp, 0, n_prompt - 1), dd, 0, 0, 0))],
        out_shape=[jax.ShapeDtypeStruct((2, m, di), bf16),
                   jax.ShapeDtypeStruct((n_prompt, 2, nh, hp, ns), f32)],
        scratch_shapes=[pltpu.VMEM((ns, di), f32)] + [pltpu.VMEM((nh, lc), f32)] * 4
                       + [pltpu.VMEM((ng, lc, nh // ng), f32)],
        compiler_params=_params(2), name="ssd_scan",
    )(xs, bc, dt, a_log.reshape(nl, 1, ndt), h0)

    tg = _tile(rows.seq_s, 256)
    vrow = pl.BlockSpec((tg, di), lambda i: (i, 0))
    gated = pl.pallas_call(
        _ssd_gate_kernel, grid=(m // tg,),
        in_specs=[pl.BlockSpec((2, tg, di), lambda i: (0, i, 0)), vrow, vrow,
                  pl.BlockSpec((None, 1, di), lambda i: (j, 0, 0)),
                  pl.BlockSpec((None, 1, di), lambda i: (j, 0, 0))],
        out_specs=vrow, out_shape=jax.ShapeDtypeStruct((m, di), bf16),
        compiler_params=_params(1), name="ssd_gate",
    )(y2, xs, z, jnp.repeat(d_skip, hp, axis=1).reshape(nl, 1, di), norm_g.reshape(nl, 1, di))
    tmo, tno = _tile(m, 512), _tile(d, 512)
    out = _fused_matmul(gated, w_out, j, [0], tm=tmo, tn=tno, n_tiles=d // tno, extras=[],
                        outs=[_out_tile(m, d, tmo, tno, BRANCH_DTYPE)], epilogue=_epi_plain, name="ssd_out")[0]
    return out, new_state


def _norm_rope(x, cos_g, sin_g, even_lane):
    n = x.shape[-1]
    partner = jnp.where(even_lane, pltpu.roll(x, n - 1, axis=1), pltpu.roll(x, 1, axis=1))
    r = lax.rsqrt(jnp.mean(x * x, axis=-1, keepdims=True) + EPS)
    return (x * cos_g + partner * sin_g) * r, r


def _q_epilogue(acc_refs, e_refs, o_refs, i, *, hd):
    cos_ref, sin_ref = e_refs
    acc_ref, o_ref = acc_refs[0], o_refs[0]
    tm, tn = acc_ref.shape
    even_lane = (lax.broadcasted_iota(jnp.int32, (QK_ROWS, hd), 1) & 1) == 0

    def rows(ridx, carry):
        rs = pl.ds(pl.multiple_of(ridx * QK_ROWS, QK_ROWS), QK_ROWS)
        cos_g, sin_g = cos_ref[rs, :], sin_ref[rs, :]
        for hh in range(tn // hd):
            q, _ = _norm_rope(acc_ref[rs, hh * hd:(hh + 1) * hd], cos_g, sin_g, even_lane)
            o_ref[rs, hh * hd:(hh + 1) * hd] = q.astype(o_ref.dtype)
        return carry

    lax.fori_loop(0, tm // QK_ROWS, rows, 0)


def _kv_epilogue(acc_refs, e_refs, o_refs, i, *, hd, nkv):
    gain_ref, cos_ref, sin_ref = e_refs
    acc_ref = acc_refs[0]
    kv_ref, kv_f32_ref = o_refs
    tm = acc_ref.shape[0]
    gain = jnp.tile(jnp.broadcast_to(gain_ref[...], (ROW_HALO, hd)), (QK_ROWS // ROW_HALO, 1))
    even_lane = (lax.broadcasted_iota(jnp.int32, (QK_ROWS, hd), 1) & 1) == 0

    def rows(ridx, carry):
        rs = pl.ds(pl.multiple_of(ridx * QK_ROWS, QK_ROWS), QK_ROWS)
        cos_g, sin_g = cos_ref[rs, :], sin_ref[rs, :]
        for hh in range(nkv):
            x = acc_ref[rs, hh * hd:(hh + 1) * hd]
            k_rot, r = _norm_rope(x, cos_g, sin_g, even_lane)
            kv_f32_ref[rs, hh * hd:(hh + 1) * hd] = x * r * gain
            kv_ref[rs, hh * hd:(hh + 1) * hd] = k_rot.astype(kv_ref.dtype)
        v = acc_ref[rs, nkv * hd:]
        kv_f32_ref[rs, nkv * hd:] = v
        kv_ref[rs, nkv * hd:] = v.astype(kv_ref.dtype)
        return carry

    lax.fori_loop(0, tm // QK_ROWS, rows, 0)


def _attn_kernel(q_ref, k_ref, v_ref, *rest, group, hd, cached):
    if cached:
        ck_ref, cv_ref, o_ref = rest
        ck = ck_ref[...].astype(bf16)
        cv = cv_ref[...].astype(bf16)
    else:
        (o_ref,) = rest
    k = k_ref[...]
    v = v_ref[...]
    nt = (((1,), (1,)), ((), ()))
    for g in range(group):
        q = q_ref[:, g * hd:(g + 1) * hd]
        s = lax.dot_general(q, k, nt, preferred_element_type=f32)
        mx = jnp.max(s, axis=-1, keepdims=True)
        if cached:
            s2 = lax.dot_general(q, ck, nt, preferred_element_type=f32)
            mx = jnp.maximum(mx, jnp.max(s2, axis=-1, keepdims=True))
        p = jnp.exp(s - mx)
        den = jnp.sum(p, axis=-1, keepdims=True)
        o = jnp.dot(p.astype(bf16), v, preferred_element_type=f32)
        if cached:
            p2 = jnp.exp(s2 - mx)
            den = den + jnp.sum(p2, axis=-1, keepdims=True)
            o = o + jnp.dot(p2.astype(bf16), cv, preferred_element_type=f32)
        o_ref[:, g * hd:(g + 1) * hd] = (o / den).astype(o_ref.dtype)


def _attention(h, j, cache_k, cache_v, w_qkv, q_norm, k_norm, w_o, rows, n_prompt):
    m, d = h.shape
    nl = w_qkv.shape[0]
    hd = q_norm.shape[-1]
    n_seq_s, _, past, nkv, _ = cache_k.shape
    nq = w_o.shape[1] // hd
    group = nq // nkv
    assert rows.seq_s % GRID_W == 0
    pos = jnp.arange(rows.seq_s)
    inv = ROPE_THETA ** (-jnp.arange(hd // 4, dtype=f32) / (hd // 4))
    ang = jnp.concatenate([(pos // GRID_W).astype(f32)[:, None] * inv, (pos % GRID_W).astype(f32)[:, None] * inv], -1)
    cos_s = jnp.repeat(jnp.cos(ang), 2, axis=-1)
    sin_s = jnp.stack([-jnp.sin(ang), jnp.sin(ang)], axis=-1).reshape(rows.seq_s, hd)
    reps = rows.ms // rows.seq_s
    cos_t = jnp.concatenate([jnp.ones((rows.mp, hd), f32), jnp.tile(cos_s, (reps, 1))], 0)
    sin_t = jnp.concatenate([jnp.zeros((rows.mp, hd), f32), jnp.tile(sin_s, (reps, 1))], 0)
    pair_swap = lambda g: g.reshape(hd // 2, 2)[:, ::-1].reshape(hd)
    gq, gk = q_norm[j] * hd ** -0.5, k_norm[j]
    cos_q, sin_q = cos_t * gq, sin_t * pair_swap(gq)
    cos_k, sin_k = cos_t * gk, sin_t * pair_swap(gk)

    tm = _tile(rows.seq_s, 1024)
    tab = pl.BlockSpec((tm, hd), lambda jj, i: (i, 0))
    gspec = pl.BlockSpec((None, 1, hd), lambda jj, i: (j, 0, 0))
    nqc, nkc = nq * hd, nkv * hd
    tnq = _tile(nqc, 1024)
    q = _fused_matmul(h, w_qkv, j, [0], tm=tm, tn=tnq, n_tiles=nqc // tnq,
                      extras=[(cos_q, tab), (sin_q, tab)],
                      outs=[_out_tile(m, nqc, tm, tnq, bf16)],
                      epilogue=functools.partial(_q_epilogue, hd=hd), buffered=True, name="attn_q")[0]
    tnk = 2 * nkc
    assert nqc % tnk == 0
    kv, kv_f32 = _fused_matmul(h, w_qkv, j, [nqc // tnk], tm=tm, tn=tnk, n_tiles=1,
                               extras=[(k_norm.reshape(nl, 1, hd), gspec), (cos_k, tab), (sin_k, tab)],
                               outs=[_out_tile(m, tnk, tm, tnk, bf16), _out_tile(m, tnk, tm, tnk, f32)],
                               epilogue=functools.partial(_kv_epilogue, hd=hd, nkv=nkv), buffered=True,
                               name="attn_kv")

    gw = group * hd
    sp = rows.seq_p
    o_p = pl.pallas_call(
        functools.partial(_attn_kernel, group=group, hd=hd, cached=False),
        grid=(n_prompt, nkv),
        in_specs=[pl.BlockSpec((sp, gw), lambda b, kh: (b, kh)),
                  pl.BlockSpec((sp, hd), lambda b, kh: (b, kh)),
                  pl.BlockSpec((sp, hd), lambda b, kh: (b, nkv + kh))],
        out_specs=pl.BlockSpec((sp, gw), lambda b, kh: (b, kh)),
        out_shape=jax.ShapeDtypeStruct((rows.mp, nqc), bf16),
        compiler_params=_params(2), name="attn_prompt",
    )(q, kv, kv)
    ss = rows.seq_s
    tq = _tile(ss, 512)
    nqb = ss // tq
    pb = rows.mp // ss
    pbq = rows.mp // tq
    ck = cache_k[:, j].reshape(n_seq_s, past, nkc)
    cv = cache_v[:, j].reshape(n_seq_s, past, nkc)
    o_s = pl.pallas_call(
        functools.partial(_attn_kernel, group=group, hd=hd, cached=True),
        grid=(n_seq_s, nkv, nqb),
        in_specs=[pl.BlockSpec((tq, gw), lambda b, kh, qi: (pbq + b * nqb + qi, kh)),
                  pl.BlockSpec((ss, hd), lambda b, kh, qi: (pb + b, kh)),
                  pl.BlockSpec((ss, hd), lambda b, kh, qi: (pb + b, nkv + kh)),
                  pl.BlockSpec((None, past, hd), lambda b, kh, qi: (b, 0, kh)),
                  pl.BlockSpec((None, past, hd), lambda b, kh, qi: (b, 0, kh))],
        out_specs=pl.BlockSpec((tq, gw), lambda b, kh, qi: (b * nqb + qi, kh)),
        out_shape=jax.ShapeDtypeStruct((rows.ms, nqc), bf16),
        compiler_params=_params(3), name="attn_sample",
    )(q, kv, kv, ck, cv)
    o = jnp.concatenate([o_p, o_s], axis=0)
    tmo, tno = _tile(m, 1024), _tile(d, 1024)
    out = _fused_matmul(o, w_o, j, [0], tm=tmo, tn=tno, n_tiles=d // tno, extras=[],
                        outs=[_out_tile(m, d, tmo, tno, BRANCH_DTYPE)], epilogue=_epi_plain, name="attn_o")[0]
    new_k = kv_f32[:rows.mp, :nkc].reshape(n_prompt, sp, nkv, hd)
    new_v = kv_f32[:rows.mp, nkc:].reshape(n_prompt, sp, nkv, hd)
    return out, new_k, new_v


def kernel(x_prompt, x_sample, c, state_ssd, cache_k, cache_v, c_ctx, w_mod, b_mod, norm_pre, norm_post, cv_w_pw1, cv_b_pw1, cv_w_dw, cv_b_dw, cv_ln_g, cv_ln_b, cv_w_pw2, cv_b_pw2, ssd_w_in, ssd_w_conv, ssd_b_conv, ssd_dt_bias, ssd_a_log, ssd_d, ssd_norm_g, ssd_w_out, attn_w_qkv, attn_q_norm, attn_k_norm, attn_w_o, ffn_w_up, ffn_w_dw, ffn_b_dw, ffn_w_down):
    nb, sp, d = x_prompt.shape
    ndb, ss, _ = x_sample.shape
    depth = w_mod.shape[0]
    rows = _Rows(nb * sp, ndb * ss, sp, ss)
    assert sp & (sp - 1) == 0 and ss & (ss - 1) == 0, "sequence lengths must be powers of two"
    assert rows.mp % ss == 0 and 1 + ndb <= COND_ROWS

    y = (x_prompt.reshape(rows.mp, d), x_sample.reshape(rows.ms, d))
    cond = jnp.concatenate([c_ctx[None, :], c, jnp.zeros((COND_ROWS - 1 - ndb, d), f32)], axis=0)
    mod = _modulation(cond, w_mod, b_mod).reshape(depth * COND_ROWS, 6, d)
    gpre = norm_pre.reshape(depth * 2, 1, d)
    gpost = norm_post.reshape(depth * 2, 1, d)
    h = _prenorm(y, gpre, 0, mod, 0, 1, 0, rows)
    new_ssd, new_k, new_v = [], [], []
    for i in range(depth):
        kind, j = i % N_MIXERS, i // N_MIXERS
        if kind == 0:
            o = _conformer(h, j, cv_w_pw1, cv_b_pw1, cv_w_dw, cv_b_dw, cv_ln_g, cv_ln_b, cv_w_pw2, cv_b_pw2, rows)
        elif kind == 1:
            o, st = _ssd_mixer(h, j, state_ssd, ssd_w_in, ssd_w_conv, ssd_b_conv, ssd_dt_bias, ssd_a_log,
                               ssd_d, ssd_norm_g, ssd_w_out, rows, nb)
            new_ssd.append(st)
        else:
            o, k_new, v_new = _attention(h, j, cache_k, cache_v, attn_w_qkv, attn_q_norm, attn_k_norm,
                                         attn_w_o, rows, nb)
            new_k.append(k_new)
            new_v.append(v_new)
        y, h = _resid(y, o, gpost, 2 * i, mod, i, 2, rows, nxt=(gpre, 2 * i + 1, i, 4, 3))
        o = _conv_ffn(h, i, ffn_w_up, ffn_w_dw, ffn_b_dw, ffn_w_down, rows)
        if i + 1 < depth:
            y, h = _resid(y, o, gpost, 2 * i + 1, mod, i, 5, rows, nxt=(gpre, 2 * i + 2, i + 1, 1, 0))
        else:
            yp, ys = _resid(y, o, gpost, 2 * i + 1, mod, i, 5, rows, split_out=True)
    yp = yp.reshape(nb, sp, d)
    ys = ys.reshape(ndb, ss, d)
    return (yp, ys, jnp.stack(new_ssd, axis=1), jnp.stack(new_k, axis=1), jnp.stack(new_v, axis=1))
```

```python
import functools
import math

import jax
import jax.numpy as jnp
from jax import lax
from jax.experimental import pallas as pl
from jax.experimental.pallas import tpu as pltpu

f32 = jnp.float32
bf16 = jnp.bfloat16
BRANCH_DTYPE = bf16

EPS = 1e-6
GRID_W = 64
ROPE_THETA = 10000.0
SSM_CHUNK = 128
N_MIXERS = 3
V7X_VMEM_LIMIT_BYTES = 56 * 1024 * 1024
COND_ROWS = 8
ROW_HALO = 8
ROW_CHUNK = 256
CONV_BLOCK = 256
SLAB_LANES = 128
EPILOGUE_LAG = 1


def _params(n_axes, vmem=V7X_VMEM_LIMIT_BYTES):
    return pltpu.CompilerParams(dimension_semantics=("arbitrary",) * n_axes,
                                vmem_limit_bytes=vmem)


def _tile(dim, pref):
    return pref if dim % pref == 0 else dim


def _sigmoid(x):
    return 1.0 / (1.0 + jnp.exp(-x))


def _silu(x):
    return x * _sigmoid(x)


def _rms(x, g):
    return x * lax.rsqrt(jnp.mean(x * x, axis=-1, keepdims=True) + EPS) * g


def _mod_kernel(c_ref, w_ref, b_ref, o_ref):
    s = _silu(c_ref[...]).astype(bf16)
    o_ref[...] = jnp.dot(s, w_ref[...].astype(bf16), preferred_element_type=f32) + b_ref[...]


def _modulation(cond, w_mod, b_mod):
    depth, d, n = w_mod.shape
    tn = _tile(n, 2048)
    return pl.pallas_call(
        _mod_kernel,
        grid=(depth, n // tn),
        in_specs=[pl.BlockSpec((COND_ROWS, d), lambda l, j: (0, 0)),
                  pl.BlockSpec((None, d, tn), lambda l, j: (l, 0, j)),
                  pl.BlockSpec((None, 1, tn), lambda l, j: (l, 0, j))],
        out_specs=pl.BlockSpec((None, COND_ROWS, tn), lambda l, j: (l, 0, j)),
        out_shape=jax.ShapeDtypeStruct((depth, COND_ROWS, n), f32),
        compiler_params=_params(2), name="modulation",
    )(cond, w_mod, b_mod.reshape(depth, 1, n))


def _prenorm_kernel(yp_ref, ys_ref, g_ref, mod_ref, h_ref, *, sc, sh, n_prompt_tiles):
    y = jnp.where(pl.program_id(0) < n_prompt_tiles, yp_ref[...], ys_ref[...])
    yn = _rms(y, g_ref[...])
    h_ref[...] = (yn * (1.0 + mod_ref[sc:sc + 1, :]) + mod_ref[sh:sh + 1, :]).astype(h_ref.dtype)


def _resid_kernel(*refs, ga, nxt, split_in, split_out, n_prompt_tiles):
    i = pl.program_id(0)
    if split_in:
        yp_ref, ys_ref, o_ref, gpost_ref, moda_ref, *rest = refs
        y_in = jnp.where(i < n_prompt_tiles, yp_ref[...], ys_ref[...])
    else:
        y_ref, o_ref, gpost_ref, moda_ref, *rest = refs
        y_in = y_ref[...]
    y = y_in + moda_ref[ga:ga + 1, :] * _rms(o_ref[...].astype(f32), gpost_ref[...])
    if split_out:
        outp_ref, outs_ref = rest

        @pl.when(i < n_prompt_tiles)
        def _():
            outp_ref[...] = y

        @pl.when(i >= n_prompt_tiles)
        def _():
            outs_ref[...] = y
    elif nxt is None:
        (ynew_ref,) = rest
        ynew_ref[...] = y
    else:
        gpre_ref, modb_ref, ynew_ref, h_ref = rest
        sc, sh = nxt
        ynew_ref[...] = y
        yn = _rms(y, gpre_ref[...])
        h_ref[...] = (yn * (1.0 + modb_ref[sc:sc + 1, :]) + modb_ref[sh:sh + 1, :]).astype(h_ref.dtype)


class _Rows:
    def __init__(self, mp, ms, seq_p, seq_s):
        self.mp, self.ms, self.m = mp, ms, mp + ms
        self.seq_p, self.seq_s = seq_p, seq_s
        self.chunk = _tile(seq_p, ROW_CHUNK)
        assert seq_p % self.chunk == 0 and seq_s % self.chunk == 0

    def cond_of_tile(self, tm):
        assert self.mp % tm == 0 and self.seq_s % tm == 0
        npt, per = self.mp // tm, self.seq_s // tm
        return lambda i: jnp.where(i < npt, 0, 1 + (i - npt) // per)


def _split_specs(rows, tm, d):
    npt = rows.mp // tm
    return [pl.BlockSpec((tm, d), lambda i: (jnp.minimum(i, npt - 1), 0)),
            pl.BlockSpec((tm, d), lambda i: (jnp.maximum(i - npt, 0), 0))]


def _prenorm(y_pair, g_all, g_idx, mod, layer, sc, sh, rows):
    d = y_pair[0].shape[1]
    m = rows.m
    tm = _tile(rows.seq_s, 512)
    cond = rows.cond_of_tile(tm)
    return pl.pallas_call(
        functools.partial(_prenorm_kernel, sc=sc, sh=sh, n_prompt_tiles=rows.mp // tm),
        grid=(m // tm,),
        in_specs=_split_specs(rows, tm, d)
                 + [pl.BlockSpec((None, 1, d), lambda i: (g_idx, 0, 0)),
                    pl.BlockSpec((None, 6, d), lambda i: (layer * COND_ROWS + cond(i), 0, 0))],
        out_specs=pl.BlockSpec((tm, d), lambda i: (i, 0)),
        out_shape=jax.ShapeDtypeStruct((m, d), bf16),
        compiler_params=_params(1), name="prenorm",
    )(*y_pair, g_all, mod)


def _resid(y, o, gpost_all, gpost_idx, mod, layer, ga, rows, nxt=None, split_out=False):
    m, d = o.shape
    tm = _tile(rows.seq_s, 512)
    cond = rows.cond_of_tile(tm)
    row_spec = pl.BlockSpec((tm, d), lambda i: (i, 0))
    split_in = isinstance(y, tuple)
    in_specs = (_split_specs(rows, tm, d) if split_in else [row_spec]) + [
        row_spec,
        pl.BlockSpec((None, 1, d), lambda i: (gpost_idx, 0, 0)),
        pl.BlockSpec((None, 6, d), lambda i: (layer * COND_ROWS + cond(i), 0, 0))]
    args = (list(y) if split_in else [y]) + [o, gpost_all, mod]
    if split_out:
        assert nxt is None
        out_specs = _split_specs(rows, tm, d)
        out_shape = [jax.ShapeDtypeStruct((rows.mp, d), f32), jax.ShapeDtypeStruct((rows.ms, d), f32)]
        kn = None
    elif nxt is None:
        out_specs, out_shape, kn = row_spec, jax.ShapeDtypeStruct((m, d), f32), None
    else:
        gpre_all, gpre_idx, nlayer, sc, sh = nxt
        in_specs += [pl.BlockSpec((None, 1, d), lambda i: (gpre_idx, 0, 0)),
                     pl.BlockSpec((None, 6, d), lambda i: (nlayer * COND_ROWS + cond(i), 0, 0))]
        args += [gpre_all, mod]
        out_specs = [row_spec, row_spec]
        out_shape = [jax.ShapeDtypeStruct((m, d), f32), jax.ShapeDtypeStruct((m, d), bf16)]
        kn = (sc, sh)
    return pl.pallas_call(
        functools.partial(_resid_kernel, ga=ga, nxt=kn, split_in=split_in, split_out=split_out,
                          n_prompt_tiles=rows.mp // tm),
        grid=(m // tm,), in_specs=in_specs, out_specs=out_specs, out_shape=out_shape,
        compiler_params=_params(1), name="resid",
    )(*args)


def _fmm_kernel(*refs, n_slab, n_extra, n_out, epilogue, chunk):
    x_ref = refs[0]
    w_refs = refs[1:1 + n_slab]
    e_refs = refs[1 + n_slab:1 + n_slab + n_extra]
    o_refs = refs[1 + n_slab + n_extra:1 + n_slab + n_extra + n_out]
    wb_refs = refs[1 + n_slab + n_extra + n_out:1 + 2 * n_slab + n_extra + n_out]
    u_refs = refs[1 + 2 * n_slab + n_extra + n_out:]
    i = pl.program_id(1)

    @pl.when(i == 0)
    def _():
        for w_ref, wb_ref in zip(w_refs, wb_refs):
            wb_ref[...] = w_ref[...].astype(bf16)

    if chunk is None:
        x = x_ref[...]
        accs = [jnp.dot(x, wb_ref[...], preferred_element_type=f32) for wb_ref in wb_refs]
        if u_refs:
            for u_ref, acc in zip(u_refs, accs):
                u_ref[...] = acc
            accs = u_refs
        epilogue(accs, e_refs, o_refs, i)
        return

    tm = x_ref.shape[0]
    n_chunk = tm // chunk
    per_slab = [u_refs[s * n_chunk:(s + 1) * n_chunk] for s in range(n_slab)]
    zeros = jnp.zeros((ROW_HALO, u_refs[0].shape[1]), f32)
    for bufs in per_slab:
        bufs[0][0:ROW_HALO, :] = zeros
        bufs[-1][ROW_HALO + chunk:, :] = zeros
    for c in range(n_chunk + EPILOGUE_LAG):
        if c < n_chunk:
            xc = x_ref[c * chunk:(c + 1) * chunk, :]
            for wb_ref, bufs in zip(wb_refs, per_slab):
                prod = jnp.dot(xc, wb_ref[...], preferred_element_type=f32)
                bufs[c][ROW_HALO:ROW_HALO + chunk, :] = prod
                if c > 0:
                    bufs[c - 1][ROW_HALO + chunk:, :] = prod[0:ROW_HALO]
                if c + 1 < n_chunk:
                    bufs[c + 1][0:ROW_HALO, :] = prod[chunk - ROW_HALO:]
        if c >= EPILOGUE_LAG:
            done = c - EPILOGUE_LAG
            epilogue(done, [bufs[done] for bufs in per_slab], e_refs, o_refs, i)


def _fused_matmul(x, w_all, layer, slab_offs, *, tm, tn, n_tiles, extras, outs, epilogue, name, chunk=None,
                  buffered=False):
    m, k = x.shape
    in_specs = [pl.BlockSpec((tm, k), lambda j, i: (i, 0))]
    args = [x]
    for off in slab_offs:
        in_specs.append(pl.BlockSpec((None, k, tn), lambda j, i, off=off: (layer, 0, off + j)))
        args.append(w_all)
    for arr, spec in extras:
        in_specs.append(spec)
        args.append(arr)
    scratch = [pltpu.VMEM((k, tn), bf16) for _ in slab_offs]
    if chunk is not None:
        assert tm % chunk == 0 and not buffered
        scratch += [pltpu.VMEM((chunk + 2 * ROW_HALO, tn), f32)] * (len(slab_offs) * (tm // chunk))
    if buffered:
        scratch += [pltpu.VMEM((tm, tn), f32) for _ in slab_offs]
    return pl.pallas_call(
        functools.partial(_fmm_kernel, n_slab=len(slab_offs), n_extra=len(extras),
                          n_out=len(outs), epilogue=epilogue, chunk=chunk),
        grid=(n_tiles, m // tm),
        in_specs=in_specs,
        out_specs=[spec for _, spec in outs],
        out_shape=[sds for sds, _ in outs],
        scratch_shapes=scratch,
        compiler_params=_params(2), name=name,
    )(*args)


def _bias_spec(layer, tn, off=0):
    return pl.BlockSpec((None, 1, tn), lambda j, i: (layer, 0, off + j))


def _out_tile(m, n, tm, tn, dtype):
    return (jax.ShapeDtypeStruct((m, n), dtype), pl.BlockSpec((tm, tn), lambda j, i: (i, j)))


def _epi_plain(accs, e_refs, o_refs, i):
    o_refs[0][...] = accs[0].astype(o_refs[0].dtype)


def _epi_bias(accs, e_refs, o_refs, i):
    o_refs[0][...] = (accs[0] + e_refs[0][...]).astype(o_refs[0].dtype)


class _ChunkConv:
    def __init__(self, u_ref, w, bias, c, chunk, seq):
        self.u_ref, self.chunk = u_ref, chunk
        self.width = w.shape[0]
        self.left = (self.width - 1) // 2
        assert self.left <= ROW_HALO and self.width - 1 - self.left <= ROW_HALO
        assert chunk % CONV_BLOCK == 0 and CONV_BLOCK >= 2 * ROW_HALO
        tn = u_ref.shape[1]
        self.starts_seq = ((c * chunk) & (seq - 1)) == 0
        self.ends_seq = (((c + 1) * chunk) & (seq - 1)) == 0
        self.w_tile = [jnp.broadcast_to(w[k:k + 1, :], (ROW_HALO, tn)) for k in range(self.width)]
        self.w_block = [jnp.tile(wt, (CONV_BLOCK // ROW_HALO, 1)) for wt in self.w_tile]
        self.b_tile = jnp.broadcast_to(bias, (ROW_HALO, tn))
        self.b_block = jnp.tile(self.b_tile, (CONV_BLOCK // ROW_HALO, 1))

    def _taps(self, r0, n_rows, masked):
        w_rows = self.w_tile if n_rows == ROW_HALO else self.w_block
        acc = self.b_tile if n_rows == ROW_HALO else self.b_block
        far = self.chunk + ROW_HALO
        if masked:
            row = lax.broadcasted_iota(jnp.int32, acc.shape, 0) + r0
        for k in range(self.width):
            d = k - self.left
            v = self.u_ref[pl.ds(ROW_HALO + r0 + d, n_rows), :]
            if masked and d < 0:
                v = jnp.where(row >= jnp.where(self.starts_seq, -d, -far), v, 0.0)
            if masked and d > 0:
                v = jnp.where(row < jnp.where(self.ends_seq, self.chunk - d, far), v, 0.0)
            acc = acc + v * w_rows[k]
        return acc

    def block(self, r0):
        out = self._taps(r0, CONV_BLOCK, False)
        if r0 == 0:
            out = jnp.concatenate([self._taps(0, ROW_HALO, True), out[ROW_HALO:]], axis=0)
        if r0 + CONV_BLOCK == self.chunk:
            tail = self._taps(self.chunk - ROW_HALO, ROW_HALO, True)
            out = jnp.concatenate([out[:CONV_BLOCK - ROW_HALO], tail], axis=0)
        return out


def _ffn_up_epilogue(c, u_refs, e_refs, o_refs, i, *, chunk, n_prompt_tiles, seq_p, seq_s):
    wa_ref, wg_ref, ba_ref, bg_ref = e_refs
    seq = jnp.where(i < n_prompt_tiles, seq_p, seq_s)
    conv_a = _ChunkConv(u_refs[0], wa_ref[...], ba_ref[...], c, chunk, seq)
    conv_g = _ChunkConv(u_refs[1], wg_ref[...], bg_ref[...], c, chunk, seq)
    for r0 in range(0, chunk, CONV_BLOCK):
        a, g = conv_a.block(r0), conv_g.block(r0)
        o_refs[0][c * chunk + r0:c * chunk + r0 + CONV_BLOCK, :] = (_silu(g) * a).astype(o_refs[0].dtype)


def _conv_ffn(h, layer, w_up, w_dw, b_dw, w_down, rows):
    m, d = h.shape
    dff = w_down.shape[1]
    depth = w_up.shape[0]
    tm = rows.seq_s
    tn = _tile(dff, 512)
    nt = dff // tn
    width = w_dw.shape[1]
    b3 = b_dw.reshape(depth, 1, 2 * dff)
    dw_spec = lambda off: pl.BlockSpec((None, width, tn), lambda j, i: (layer, 0, off + j))
    act = _fused_matmul(
        h, w_up, layer, [0, nt], tm=tm, tn=tn, n_tiles=nt,
        extras=[(w_dw, dw_spec(0)), (w_dw, dw_spec(nt)), (b3, _bias_spec(layer, tn)), (b3, _bias_spec(layer, tn, nt))],
        outs=[_out_tile(m, dff, tm, tn, bf16)], chunk=rows.chunk,
        epilogue=functools.partial(_ffn_up_epilogue, chunk=rows.chunk, n_prompt_tiles=rows.mp // tm,
                                   seq_p=rows.seq_p, seq_s=rows.seq_s),
        name="ffn_up")[0]
    tm2, tn2 = _tile(m, 512), _tile(d, 512)
    return _fused_matmul(act, w_down, layer, [0], tm=tm2, tn=tn2, n_tiles=d // tn2, extras=[],
                         outs=[_out_tile(m, d, tm2, tn2, BRANCH_DTYPE)], epilogue=_epi_plain, name="ffn_down")[0]


def _glu_epilogue(accs, e_refs, o_refs, i):
    a = accs[0] + e_refs[0][...]
    g = accs[1] + e_refs[1][...]
    o_refs[0][...] = (a * _sigmoid(g)).astype(o_refs[0].dtype)


CONV_HALO = 16
CONV_LANES = 128
CONV_ROWS = 128
LN_ROWS = 32
QK_ROWS = 128


def _cv_mid_kernel(u_ref, prev_ref, next_ref, w_ref, b_ref, g_ref, beta_ref, o_ref, pad_ref, conv_ref,
                   *, blocks_p, per_seq_p, per_seq_s):
    i = pl.program_id(0)
    rb, d = u_ref.shape
    width = w_ref.shape[0]
    left = (width - 1) // 2
    per = jnp.where(i < blocks_p, per_seq_p, per_seq_s)
    k_in_seq = jnp.where(i < blocks_p, i, i - blocks_p) % per
    first = k_in_seq == 0
    last = k_in_seq == per - 1
    pad_ref[0:CONV_HALO, :] = jnp.where(first, 0.0, prev_ref[...])
    pad_ref[CONV_HALO:CONV_HALO + rb, :] = u_ref[...]
    pad_ref[CONV_HALO + rb:CONV_HALO + rb + CONV_HALO, :] = jnp.where(last, 0.0, next_ref[...])

    def lane_chunk(cidx, carry):
        lo = pl.multiple_of(cidx * CONV_LANES, CONV_LANES)
        wc = w_ref[:, pl.ds(lo, CONV_LANES)]
        for r0 in range(0, rb, CONV_ROWS):
            partial = []
            for phase in range(ROW_HALO):
                ks = [k for k in range(width) if (CONV_HALO - left + k) % ROW_HALO == phase]
                if not ks:
                    continue
                span = max(CONV_HALO - left + k for k in ks) - phase
                shifted = pad_ref[pl.ds(r0 + phase, CONV_ROWS + span), pl.ds(lo, CONV_LANES)]
                acc = None
                for k in ks:
                    off = CONV_HALO - left + k - phase
                    term = shifted[off:off + CONV_ROWS] * wc[k:k + 1, :]
                    acc = term if acc is None else acc + term
                partial.append(acc)
            total = partial[0]
            for acc in partial[1:]:
                total = total + acc
            conv_ref[pl.ds(r0, CONV_ROWS), pl.ds(lo, CONV_LANES)] = total
        return carry

    lax.fori_loop(0, d // CONV_LANES, lane_chunk, 0)

    def norm_rows(ridx, carry):
        rs = pl.ds(pl.multiple_of(ridx * LN_ROWS, LN_ROWS), LN_ROWS)
        u = conv_ref[rs, :] + b_ref[...]
        mu = jnp.mean(u, axis=-1, keepdims=True)
        uc = u - mu
        v = uc * lax.rsqrt(jnp.mean(uc * uc, axis=-1, keepdims=True) + EPS) * g_ref[...] + beta_ref[...]
        o_ref[rs, :] = _silu(v).astype(o_ref.dtype)
        return carry

    lax.fori_loop(0, rb // LN_ROWS, norm_rows, 0, unroll=2)


def _conformer(h, j, w_pw1, b_pw1, w_dw, b_dw, ln_g, ln_b, w_pw2, b_pw2, rows):
    m, d = h.shape
    nl = w_pw1.shape[0]
    tm, tn = _tile(rows.seq_s, 1024), _tile(d, 512)
    nt = d // tn
    b1 = b_pw1.reshape(nl, 1, 2 * d)
    glu = _fused_matmul(h, w_pw1, j, [0, nt], tm=tm, tn=tn, n_tiles=nt,
                        extras=[(b1, _bias_spec(j, tn)), (b1, _bias_spec(j, tn, nt))],
                        outs=[_out_tile(m, d, tm, tn, f32)], epilogue=_glu_epilogue, name="cv_pw1")[0]
    width = w_dw.shape[1]
    assert (width - 1) // 2 <= CONV_HALO
    rb = _tile(rows.seq_p, 256)
    assert rows.seq_p % rb == 0 and rows.seq_s % rb == 0 and rb % CONV_ROWS == 0 and d % CONV_LANES == 0
    hb = rb // CONV_HALO
    n_halo = m // CONV_HALO
    vec = lambda a: a.reshape(nl, 1, d)
    vspec = pl.BlockSpec((None, 1, d), lambda i: (j, 0, 0))
    mid = pl.pallas_call(
        functools.partial(_cv_mid_kernel, blocks_p=rows.mp // rb, per_seq_p=rows.seq_p // rb,
                          per_seq_s=rows.seq_s // rb),
        grid=(m // rb,),
        in_specs=[pl.BlockSpec((rb, d), lambda i: (i, 0)),
                  pl.BlockSpec((CONV_HALO, d), lambda i: (jnp.maximum(i * hb - 1, 0), 0)),
                  pl.BlockSpec((CONV_HALO, d), lambda i: (jnp.minimum((i + 1) * hb, n_halo - 1), 0)),
                  pl.BlockSpec((None, width, d), lambda i: (j, 0, 0)),
                  vspec, vspec, vspec],
        out_specs=pl.BlockSpec((rb, d), lambda i: (i, 0)),
        out_shape=jax.ShapeDtypeStruct((m, d), bf16),
        scratch_shapes=[pltpu.VMEM((rb + 2 * CONV_HALO, d), f32), pltpu.VMEM((rb, d), f32)],
        compiler_params=_params(1), name="cv_mid",
    )(glu, glu, glu, w_dw, vec(b_dw), vec(ln_g), vec(ln_b))
    tm2, tn2 = _tile(m, 1024), _tile(d, 1024)
    return _fused_matmul(mid, w_pw2, j, [0], tm=tm2, tn=tn2, n_tiles=d // tn2,
                         extras=[(vec(b_pw2), _bias_spec(j, tn2))],
                         outs=[_out_tile(m, d, tm2, tn2, BRANCH_DTYPE)], epilogue=_epi_bias, name="cv_pw2")[0]


def _conv_silu_epilogue(c, u_refs, e_refs, o_refs, i, *, chunk, n_prompt_tiles, seq_p, seq_s):
    w_ref, b_ref = e_refs
    seq = jnp.where(i < n_prompt_tiles, seq_p, seq_s)
    conv = _ChunkConv(u_refs[0], w_ref[...], b_ref[...], c, chunk, seq)
    for r0 in range(0, chunk, CONV_BLOCK):
        o_refs[0][c * chunk + r0:c * chunk + r0 + CONV_BLOCK, :] = _silu(conv.block(r0)).astype(o_refs[0].dtype)


def _softplus_epilogue(accs, e_refs, o_refs, i):
    x = accs[0] + e_refs[0][...]
    o_refs[0][...] = jnp.maximum(x, 0.0) + jnp.log1p(jnp.exp(-jnp.abs(x)))


def _ssd_scan_kernel(xs_ref, bc_ref, dt_ref, alog_ref, h0_ref, y_ref, st_ref,
                     state_ref, acst_ref, dtt_ref, wt_ref, cdt_ref, acsc_ref,
                     *, nh, hp, ns, ng, tp, ncp, ncs):
    d = pl.program_id(0)
    t = pl.program_id(1)
    nt = pl.num_programs(1)
    lc = xs_ref.shape[0]
    r = nh // ng
    gp = r * hp
    fwd = d == 0
    te = jnp.where(fwd, t, nt - 1 - t)
    in_prompt = te < tp
    c_in_seq = jnp.where(in_prompt, te % ncp, (te - tp) % ncs)
    n_in_seq = jnp.where(in_prompt, ncp, ncs)
    is_first = c_in_seq == jnp.where(fwd, 0, n_in_seq - 1)
    is_last = c_in_seq == jnp.where(fwd, n_in_seq - 1, 0)

    @pl.when(jnp.logical_and(is_first, in_prompt))
    def _():
        state_ref[...] = jnp.zeros_like(state_ref)

    @pl.when(jnp.logical_and(is_first, jnp.logical_not(in_prompt)))
    def _():
        for g in range(ng):
            state_ref[:, g * gp:(g + 1) * gp] = h0_ref[g * r:(g + 1) * r].reshape(gp, ns).T

    dt2 = dt_ref[...]
    da2 = dt2 * (-jnp.exp(alog_ref[...]))
    row = lax.broadcasted_iota(jnp.int32, (lc, lc), 0)
    col = lax.broadcasted_iota(jnp.int32, (lc, lc), 1)
    diff = jnp.where(fwd, row - col, col - row)
    mask = diff >= 0
    tri = jnp.where(mask, 1.0, 0.0).astype(bf16)
    hi = da2.astype(bf16)
    rest = da2 - hi.astype(f32)
    mid = rest.astype(bf16)
    low = (rest - mid.astype(f32)).astype(bf16)
    acs2 = (jnp.dot(tri, hi, preferred_element_type=f32) + jnp.dot(tri, mid, preferred_element_type=f32)
            + jnp.dot(tri, low, preferred_element_type=f32))
    acst2 = acs2.T
    dtt2 = dt2.T
    acs = jnp.where(fwd, acs2[:, :nh], acs2[:, nh:])
    acst = jnp.where(fwd, acst2[:nh], acst2[nh:])
    dtt = jnp.where(fwd, dtt2[:nh], dtt2[nh:])
    total = jnp.where(fwd, acst[:, lc - 1:lc], acst[:, 0:1])
    totb = jnp.broadcast_to(total, (nh, lc))
    acst_ref[...] = acst
    dtt_ref[...] = dtt
    wt_ref[...] = dtt * jnp.exp(totb - acst)
    cdt_ref[...] = jnp.exp(totb)
    for g in range(ng):
        acsc_ref[g] = acs[:, g * r:(g + 1) * r]
    pair = SLAB_LANES // hp
    first_lanes = lax.broadcasted_iota(jnp.int32, (1, SLAB_LANES), 1) < hp

    def group(g, carry):
        lo = pl.multiple_of(g * gp, gp)
        bg = bc_ref[:, pl.ds(pl.multiple_of(g * ns, ns), ns)]
        cg = bc_ref[:, pl.ds(pl.multiple_of((ng + g) * ns, ns), ns)]
        scores = lax.dot_general(cg, bg, (((1,), (1,)), ((), ())), preferred_element_type=f32)
        bt = bg.astype(f32).T
        cf = cg.astype(f32)
        cols = acsc_ref[g]
        hrow = pl.ds(pl.multiple_of(g * r, r), r)
        rows_a, rows_dt, rows_w, rows_cd = acst_ref[hrow, :], dtt_ref[hrow, :], wt_ref[hrow, :], cdt_ref[hrow, :]
        glanes = pl.ds(lo, gp)
        x_g = xs_ref[:, glanes].astype(bf16)
        st_g = state_ref[:, glanes]
        ys, sts = [], []
        for k in range(r // pair):
            st = st_g[:, k * SLAB_LANES:(k + 1) * SLAB_LANES]
            x_k = x_g[:, k * SLAB_LANES:(k + 1) * SLAB_LANES]
            tops, bws = [], []
            for q in (pair * k, pair * k + 1):
                a_l = jnp.broadcast_to(cols[:, q:q + 1], (lc, lc))
                seg = a_l - rows_a[q:q + 1, :]
                mh = scores * jnp.where(mask, jnp.exp(seg), 0.0) * rows_dt[q:q + 1, :]
                ch = cf * jnp.exp(a_l)
                tops.append(jnp.concatenate([mh.astype(bf16), ch.astype(bf16)], axis=1))
                bws.append((bt * rows_w[q:q + 1, :]).astype(bf16))
            out = jnp.dot(jnp.concatenate(tops, axis=0), jnp.concatenate([x_k, st.astype(bf16)], axis=0),
                          preferred_element_type=f32)
            y = jnp.where(first_lanes, out[:lc], out[lc:])
            x_0 = jnp.where(first_lanes, x_k, jnp.zeros_like(x_k))
            x_1 = jnp.where(first_lanes, jnp.zeros_like(x_k), x_k)
            cs = jnp.dot(jnp.concatenate(bws, axis=1), jnp.concatenate([x_0, x_1], axis=0),
                         preferred_element_type=f32)
            cd = jnp.where(first_lanes, rows_cd[pair * k:pair * k + 1, :], rows_cd[pair * k + 1:pair * k + 2, :])
            sts.append(st * cd + cs)
            ys.append(y.astype(y_ref.dtype))
        state_ref[:, glanes] = jnp.concatenate(sts, axis=1)
        y_ref[:, glanes] = jnp.concatenate(ys, axis=1)
        return carry

    lax.fori_loop(0, ng, group, 0, unroll=8)

    @pl.when(jnp.logical_and(is_last, in_prompt))
    def _():
        for g in range(ng):
            st_ref[g * r:(g + 1) * r] = state_ref[:, g * gp:(g + 1) * gp].T.reshape(r, hp, ns)


def _ssd_gate_kernel(y_ref, xs_ref, z_ref, dx_ref, g_ref, o_ref):
    y = y_ref[0].astype(f32) + y_ref[1].astype(f32) + dx_ref[...] * xs_ref[...].astype(f32)
    y = y * _silu(z_ref[...].astype(f32))
    o_ref[...] = _rms(y, g_ref[...]).astype(o_ref.dtype)


def _ssd_mixer(h, j, h0, w_in, w_conv, b_conv, dt_bias, a_log, d_skip, norm_g, w_out, rows, n_prompt):
    m, d = h.shape
    nl = w_in.shape[0]
    n_seq_s, _, _, nh, hp, ns = h0.shape
    di = nh * hp
    conv_dim = w_conv.shape[-1]
    ng = (conv_dim - di) // (2 * ns)
    lc = SSM_CHUNK
    assert rows.seq_p % lc == 0 and rows.seq_s % lc == 0 and nh % ng == 0
    tm = rows.seq_s
    npt = rows.mp // tm

    tnz = _tile(di, 1024)
    tmz = _tile(rows.seq_s, 1024)
    z = _fused_matmul(h, w_in, j, [0], tm=tmz, tn=tnz, n_tiles=di // tnz, extras=[],
                      outs=[_out_tile(m, di, tmz, tnz, bf16)], epilogue=_epi_plain, name="ssd_in_z")[0]
    tnx = _tile(di, 512)
    assert di % tnx == 0 and (2 * ng * ns) % tnx == 0
    cw = w_conv.shape[1]
    bcv = b_conv.reshape(nl, 1, conv_dim)
    conv_epi = functools.partial(_conv_silu_epilogue, chunk=rows.chunk, n_prompt_tiles=npt,
                                 seq_p=rows.seq_p, seq_s=rows.seq_s)

    def conv_part(col0, width_cols, dtype, name):
        nt = width_cols // tnx
        woff = (di + col0) // tnx
        coff = col0 // tnx
        return _fused_matmul(
            h, w_in, j, [woff], tm=tm, tn=tnx, n_tiles=nt,
            extras=[(w_conv, pl.BlockSpec((None, cw, tnx), lambda jj, i: (j, 0, coff + jj))),
                    (bcv, _bias_spec(j, tnx, coff))],
            outs=[_out_tile(m, width_cols, tm, tnx, dtype)], epilogue=conv_epi, chunk=rows.chunk, name=name)[0]

    xs = conv_part(0, di, bf16, "ssd_in_x")
    bc = conv_part(di, 2 * ng * ns, bf16, "ssd_in_bc")
    ndt = 2 * nh
    dt = _fused_matmul(h, w_in, j, [(di + conv_dim) // ndt], tm=tmz, tn=ndt, n_tiles=1,
                       extras=[(dt_bias.reshape(nl, 1, ndt), _bias_spec(j, ndt))],
                       outs=[_out_tile(m, ndt, tmz, ndt, f32)], epilogue=_softplus_epilogue, name="ssd_in_dt")[0]

    tp, ts = rows.mp // lc, rows.ms // lc
    ncp, ncs = rows.seq_p // lc, rows.seq_s // lc
    nt = tp + ts
    assert lc == SLAB_LANES and ns == lc and 2 * hp == SLAB_LANES and (nh // ng) % 2 == 0 and ng % 2 == 0
    chunk = lambda dd, t: jnp.where(dd == 0, t, nt - 1 - t)
    rowmap = lambda dd, t: (chunk(dd, t), 0)
    y2, new_state = pl.pallas_call(
        functools.partial(_ssd_scan_kernel, nh=nh, hp=hp, ns=ns, ng=ng, tp=tp, ncp=ncp, ncs=ncs),
        grid=(2, nt),
        in_specs=[pl.BlockSpec((lc, di), rowmap),
                  pl.BlockSpec((lc, 2 * ng * ns), rowmap),
                  pl.BlockSpec((lc, ndt), rowmap),
                  pl.BlockSpec((None, 1, ndt), lambda dd, t: (j, 0, 0)),
                  pl.BlockSpec((None, None, None, nh, hp, ns),
                               lambda dd, t: (jnp.clip((chunk(dd, t) - tp) // ncs, 0, n_seq_s - 1), j, dd, 0, 0, 0))],
        out_specs=[pl.BlockSpec((None, lc, di), lambda dd, t: (dd, chunk(dd, t), 0)),
                   pl.BlockSpec((None, None, nh, hp, ns),
                                lambda dd, t: (jnp.clip(chunk(dd, t) // ncp, 0, n_prompt - 1), dd, 0, 0, 0))],
        out_shape=[jax.ShapeDtypeStruct((2, m, di), bf16),
                   jax.ShapeDtypeStruct((n_prompt, 2, nh, hp, ns), f32)],
        scratch_shapes=[pltpu.VMEM((ns, di), f32)] + [pltpu.VMEM((nh, lc), f32)] * 4
                       + [pltpu.VMEM((ng, lc, nh // ng), f32)],
        compiler_params=_params(2), name="ssd_scan",
    )(xs, bc, dt, a_log.reshape(nl, 1, ndt), h0)

    tg = _tile(rows.seq_s, 256)
    vrow = pl.BlockSpec((tg, di), lambda i: (i, 0))
    gated = pl.pallas_call(
        _ssd_gate_kernel, grid=(m // tg,),
        in_specs=[pl.BlockSpec((2, tg, di), lambda i: (0, i, 0)), vrow, vrow,
                  pl.BlockSpec((None, 1, di), lambda i: (j, 0, 0)),
                  pl.BlockSpec((None, 1, di), lambda i: (j, 0, 0))],
        out_specs=vrow, out_shape=jax.ShapeDtypeStruct((m, di), bf16),
        compiler_params=_params(1), name="ssd_gate",
    )(y2, xs, z, jnp.repeat(d_skip, hp, axis=1).reshape(nl, 1, di), norm_g.reshape(nl, 1, di))
    tmo, tno = _tile(m, 1024), _tile(d, 512)
    out = _fused_matmul(gated, w_out, j, [0], tm=tmo, tn=tno, n_tiles=d // tno, extras=[],
                        outs=[_out_tile(m, d, tmo, tno, BRANCH_DTYPE)], epilogue=_epi_plain, name="ssd_out")[0]
    return out, new_state


def _norm_rope(x, cos_g, sin_g, even_lane):
    n = x.shape[-1]
    partner = jnp.where(even_lane, pltpu.roll(x, n - 1, axis=1), pltpu.roll(x, 1, axis=1))
    r = lax.rsqrt(jnp.mean(x * x, axis=-1, keepdims=True) + EPS)
    return (x * cos_g + partner * sin_g) * r, r


def _q_epilogue(acc_refs, e_refs, o_refs, i, *, hd):
    cos_ref, sin_ref = e_refs
    acc_ref, o_ref = acc_refs[0], o_refs[0]
    tm, tn = acc_ref.shape
    even_lane = (lax.broadcasted_iota(jnp.int32, (QK_ROWS, hd), 1) & 1) == 0

    def rows(ridx, carry):
        rs = pl.ds(pl.multiple_of(ridx * QK_ROWS, QK_ROWS), QK_ROWS)
        cos_g, sin_g = cos_ref[rs, :], sin_ref[rs, :]
        for hh in range(tn // hd):
            q, _ = _norm_rope(acc_ref[rs, hh * hd:(hh + 1) * hd], cos_g, sin_g, even_lane)
            o_ref[rs, hh * hd:(hh + 1) * hd] = q.astype(o_ref.dtype)
        return carry

    lax.fori_loop(0, tm // QK_ROWS, rows, 0)


def _kv_epilogue(acc_refs, e_refs, o_refs, i, *, hd, nkv):
    gain_ref, cos_ref, sin_ref = e_refs
    acc_ref = acc_refs[0]
    kv_ref, kv_f32_ref = o_refs
    tm = acc_ref.shape[0]
    gain = jnp.tile(jnp.broadcast_to(gain_ref[...], (ROW_HALO, hd)), (QK_ROWS // ROW_HALO, 1))
    even_lane = (lax.broadcasted_iota(jnp.int32, (QK_ROWS, hd), 1) & 1) == 0

    def rows(ridx, carry):
        rs = pl.ds(pl.multiple_of(ridx * QK_ROWS, QK_ROWS), QK_ROWS)
        cos_g, sin_g = cos_ref[rs, :], sin_ref[rs, :]
        for hh in range(nkv):
            x = acc_ref[rs, hh * hd:(hh + 1) * hd]
            k_rot, r = _norm_rope(x, cos_g, sin_g, even_lane)
            kv_f32_ref[rs, hh * hd:(hh + 1) * hd] = x * r * gain
            kv_ref[rs, hh * hd:(hh + 1) * hd] = k_rot.astype(kv_ref.dtype)
        v = acc_ref[rs, nkv * hd:]
        kv_f32_ref[rs, nkv * hd:] = v
        kv_ref[rs, nkv * hd:] = v.astype(kv_ref.dtype)
        return carry

    lax.fori_loop(0, tm // QK_ROWS, rows, 0)


def _attn_kernel(q_ref, k_ref, v_ref, *rest, group, hd, cached):
    if cached:
        ck_ref, cv_ref, o_ref = rest
        ck = ck_ref[...].astype(bf16)
        cv = cv_ref[...].astype(bf16)
    else:
        (o_ref,) = rest
    k = k_ref[...]
    v = v_ref[...]
    nt = (((1,), (1,)), ((), ()))
    for g in range(group):
        q = q_ref[:, g * hd:(g + 1) * hd]
        s = lax.dot_general(q, k, nt, preferred_element_type=f32)
        mx = jnp.max(s, axis=-1, keepdims=True)
        if cached:
            s2 = lax.dot_general(q, ck, nt, preferred_element_type=f32)
            mx = jnp.maximum(mx, jnp.max(s2, axis=-1, keepdims=True))
        p = jnp.exp(s - mx)
        den = jnp.sum(p, axis=-1, keepdims=True)
        o = jnp.dot(p.astype(bf16), v, preferred_element_type=f32)
        if cached:
            p2 = jnp.exp(s2 - mx)
            den = den + jnp.sum(p2, axis=-1, keepdims=True)
            o = o + jnp.dot(p2.astype(bf16), cv, preferred_element_type=f32)
        o_ref[:, g * hd:(g + 1) * hd] = (o / den).astype(o_ref.dtype)


def _attention(h, j, cache_k, cache_v, w_qkv, q_norm, k_norm, w_o, rows, n_prompt):
    m, d = h.shape
    nl = w_qkv.shape[0]
    hd = q_norm.shape[-1]
    n_seq_s, _, past, nkv, _ = cache_k.shape
    nq = w_o.shape[1] // hd
    group = nq // nkv
    assert rows.seq_s % GRID_W == 0
    pos = jnp.arange(rows.seq_s)
    inv = ROPE_THETA ** (-jnp.arange(hd // 4, dtype=f32) / (hd // 4))
    ang = jnp.concatenate([(pos // GRID_W).astype(f32)[:, None] * inv, (pos % GRID_W).astype(f32)[:, None] * inv], -1)
    cos_s = jnp.repeat(jnp.cos(ang), 2, axis=-1)
    sin_s = jnp.stack([-jnp.sin(ang), jnp.sin(ang)], axis=-1).reshape(rows.seq_s, hd)
    reps = rows.ms // rows.seq_s
    cos_t = jnp.concatenate([jnp.ones((rows.mp, hd), f32), jnp.tile(cos_s, (reps, 1))], 0)
    sin_t = jnp.concatenate([jnp.zeros((rows.mp, hd), f32), jnp.tile(sin_s, (reps, 1))], 0)
    pair_swap = lambda g: g.reshape(hd // 2, 2)[:, ::-1].reshape(hd)
    gq, gk = q_norm[j] * hd ** -0.5, k_norm[j]
    cos_q, sin_q = cos_t * gq, sin_t * pair_swap(gq)
    cos_k, sin_k = cos_t * gk, sin_t * pair_swap(gk)

    tm = _tile(rows.seq_s, 1024)
    tab = pl.BlockSpec((tm, hd), lambda jj, i: (i, 0))
    gspec = pl.BlockSpec((None, 1, hd), lambda jj, i: (j, 0, 0))
    nqc, nkc = nq * hd, nkv * hd
    tnq = _tile(nqc, 1024)
    q = _fused_matmul(h, w_qkv, j, [0], tm=tm, tn=tnq, n_tiles=nqc // tnq,
                      extras=[(cos_q, tab), (sin_q, tab)],
                      outs=[_out_tile(m, nqc, tm, tnq, bf16)],
                      epilogue=functools.partial(_q_epilogue, hd=hd), buffered=True, name="attn_q")[0]
    tnk = 2 * nkc
    assert nqc % tnk == 0
    kv, kv_f32 = _fused_matmul(h, w_qkv, j, [nqc // tnk], tm=tm, tn=tnk, n_tiles=1,
                               extras=[(k_norm.reshape(nl, 1, hd), gspec), (cos_k, tab), (sin_k, tab)],
                               outs=[_out_tile(m, tnk, tm, tnk, bf16), _out_tile(m, tnk, tm, tnk, f32)],
                               epilogue=functools.partial(_kv_epilogue, hd=hd, nkv=nkv), buffered=True,
                               name="attn_kv")

    gw = group * hd
    sp = rows.seq_p
    o_p = pl.pallas_call(
        functools.partial(_attn_kernel, group=group, hd=hd, cached=False),
        grid=(n_prompt, nkv),
        in_specs=[pl.BlockSpec((sp, gw), lambda b, kh: (b, kh)),
                  pl.BlockSpec((sp, hd), lambda b, kh: (b, kh)),
                  pl.BlockSpec((sp, hd), lambda b, kh: (b, nkv + kh))],
        out_specs=pl.BlockSpec((sp, gw), lambda b, kh: (b, kh)),
        out_shape=jax.ShapeDtypeStruct((rows.mp, nqc), bf16),
        compiler_params=_params(2), name="attn_prompt",
    )(q, kv, kv)
    ss = rows.seq_s
    tq = _tile(ss, 512)
    nqb = ss // tq
    pb = rows.mp // ss
    pbq = rows.mp // tq
    ck = cache_k[:, j].reshape(n_seq_s, past, nkc)
    cv = cache_v[:, j].reshape(n_seq_s, past, nkc)
    o_s = pl.pallas_call(
        functools.partial(_attn_kernel, group=group, hd=hd, cached=True),
        grid=(n_seq_s, nkv, nqb),
        in_specs=[pl.BlockSpec((tq, gw), lambda b, kh, qi: (pbq + b * nqb + qi, kh)),
                  pl.BlockSpec((ss, hd), lambda b, kh, qi: (pb + b, kh)),
                  pl.BlockSpec((ss, hd), lambda b, kh, qi: (pb + b, nkv + kh)),
                  pl.BlockSpec((None, past, hd), lambda b, kh, qi: (b, 0, kh)),
                  pl.BlockSpec((None, past, hd), lambda b, kh, qi: (b, 0, kh))],
        out_specs=pl.BlockSpec((tq, gw), lambda b, kh, qi: (b * nqb + qi, kh)),
        out_shape=jax.ShapeDtypeStruct((rows.ms, nqc), bf16),
        compiler_params=_params(3), name="attn_sample",
    )(q, kv, kv, ck, cv)
    o = jnp.concatenate([o_p, o_s], axis=0)
    tmo, tno = _tile(m, 1024), _tile(d, 1024)
    out = _fused_matmul(o, w_o, j, [0], tm=tmo, tn=tno, n_tiles=d // tno, extras=[],
                        outs=[_out_tile(m, d, tmo, tno, BRANCH_DTYPE)], epilogue=_epi_plain, name="attn_o")[0]
    new_k = kv_f32[:rows.mp, :nkc].reshape(n_prompt, sp, nkv, hd)
    new_v = kv_f32[:rows.mp, nkc:].reshape(n_prompt, sp, nkv, hd)
    return out, new_k, new_v


def kernel(x_prompt, x_sample, c, state_ssd, cache_k, cache_v, c_ctx, w_mod, b_mod, norm_pre, norm_post, cv_w_pw1, cv_b_pw1, cv_w_dw, cv_b_dw, cv_ln_g, cv_ln_b, cv_w_pw2, cv_b_pw2, ssd_w_in, ssd_w_conv, ssd_b_conv, ssd_dt_bias, ssd_a_log, ssd_d, ssd_norm_g, ssd_w_out, attn_w_qkv, attn_q_norm, attn_k_norm, attn_w_o, ffn_w_up, ffn_w_dw, ffn_b_dw, ffn_w_down):
    nb, sp, d = x_prompt.shape
    ndb, ss, _ = x_sample.shape
    depth = w_mod.shape[0]
    rows = _Rows(nb * sp, ndb * ss, sp, ss)
    assert sp & (sp - 1) == 0 and ss & (ss - 1) == 0, "sequence lengths must be powers of two"
    assert rows.mp % ss == 0 and 1 + ndb <= COND_ROWS

    y = (x_prompt.reshape(rows.mp, d), x_sample.reshape(rows.ms, d))
    cond = jnp.concatenate([c_ctx[None, :], c, jnp.zeros((COND_ROWS - 1 - ndb, d), f32)], axis=0)
    mod = _modulation(cond, w_mod, b_mod).reshape(depth * COND_ROWS, 6, d)
    gpre = norm_pre.reshape(depth * 2, 1, d)
    gpost = norm_post.reshape(depth * 2, 1, d)
    h = _prenorm(y, gpre, 0, mod, 0, 1, 0, rows)
    new_ssd, new_k, new_v = [], [], []
    for i in range(depth):
        kind, j = i % N_MIXERS, i // N_MIXERS
        if kind == 0:
            o = _conformer(h, j, cv_w_pw1, cv_b_pw1, cv_w_dw, cv_b_dw, cv_ln_g, cv_ln_b, cv_w_pw2, cv_b_pw2, rows)
        elif kind == 1:
            o, st = _ssd_mixer(h, j, state_ssd, ssd_w_in, ssd_w_conv, ssd_b_conv, ssd_dt_bias, ssd_a_log,
                               ssd_d, ssd_norm_g, ssd_w_out, rows, nb)
            new_ssd.append(st)
        else:
            o, k_new, v_new = _attention(h, j, cache_k, cache_v, attn_w_qkv, attn_q_norm, attn_k_norm,
                                         attn_w_o, rows, nb)
            new_k.append(k_new)
            new_v.append(v_new)
        y, h = _resid(y, o, gpost, 2 * i, mod, i, 2, rows, nxt=(gpre, 2 * i + 1, i, 4, 3))
        o = _conv_ffn(h, i, ffn_w_up, ffn_w_dw, ffn_b_dw, ffn_w_down, rows)
        if i + 1 < depth:
            y, h = _resid(y, o, gpost, 2 * i + 1, mod, i, 5, rows, nxt=(gpre, 2 * i + 2, i + 1, 1, 0))
        else:
            yp, ys = _resid(y, o, gpost, 2 * i + 1, mod, i, 5, rows, split_out=True)
    yp = yp.reshape(nb, sp, d)
    ys = ys.reshape(ndb, ss, d)
    return (yp, ys, jnp.stack(new_ssd, axis=1), jnp.stack(new_k, axis=1), jnp.stack(new_v, axis=1))
```
